```python
import math
import jax
import jax.numpy as jnp
from jax import lax
import numpy as np

D_MODEL = 4096
BATCH = 2
SEQ = 4096
DEPTH = 4

GRID_W = 64
CTX_LEN = 256
N_MIXERS = 4
ADA_RANK = 256
FFN_HIDDEN = math.ceil(8 * D_MODEL / 3 / 256) * 256
ALPHA = (2 * DEPTH) ** 0.25
BETA = (8 * DEPTH) ** -0.25
LN_EPS = 1e-5
SSM_GROUP = 16
SSM_GROUPS = D_MODEL // SSM_GROUP
SSM_STATE = 64
SSM_CHUNK = 128
DT_MIN = 1e-3
DT_MAX = 1e-1
DA_DIM = 128
DA_HEADS = D_MODEL // (2 * DA_DIM)
Q_BLOCK = 128
ROPE_THETA = 10000.0
CM_CHUNK = 128
CM_GROUPS = 16
CM_GW = D_MODEL // CM_GROUPS
NA_DIM = 128
NA_HEADS = D_MODEL // NA_DIM
WIN_H = 8
WIN_W = 16

kernel_name = "hybrid_s5_diffattn_chunkmlp_natten_dit"


def _layernorm(x, g, b):
    xf = x.astype(jnp.float32)
    mu = jnp.mean(xf, axis=-1, keepdims=True)
    var = jnp.mean(jnp.square(xf - mu), axis=-1, keepdims=True)
    return ((xf - mu) * lax.rsqrt(var + LN_EPS)).astype(x.dtype) * g + b


def _rmsnorm(x, g):
    xf = x.astype(jnp.float32)
    y = xf * lax.rsqrt(jnp.mean(jnp.square(xf), axis=-1, keepdims=True) + LN_EPS)
    return y.astype(x.dtype) * g


def _adaln(cond, w_down, w_up, b):
    return ((jax.nn.silu(cond) @ w_down) @ w_up + b)[:, None, :]


def _swiglu(t, w_in, w_out):
    g, u = jnp.split(t @ w_in, 2, axis=-1)
    return (jax.nn.silu(g) * u) @ w_out


def _rope(x, pos):
    n = x.shape[-1]
    half = n // 2
    inv = ROPE_THETA ** (-jnp.arange(half, dtype=jnp.float32) * 2.0 / n)
    ang = pos.astype(jnp.float32)[:, None] * inv
    ang = ang.reshape((pos.shape[0],) + (1,) * (x.ndim - 3) + (half,))
    cos = jnp.cos(ang).astype(x.dtype)
    sin = jnp.sin(ang).astype(x.dtype)
    x1, x2 = x[..., :half], x[..., half:]
    return jnp.concatenate([x1 * cos - x2 * sin, x1 * sin + x2 * cos], axis=-1)


def _rope_2d(x, row, col):
    half = x.shape[-1] // 2
    return jnp.concatenate([_rope(x[..., :half], row), _rope(x[..., half:], col)], axis=-1)


def _lin_combine(e1, e2):
    a1, b1 = e1
    a2, b2 = e2
    return a1 * a2, a2 * b1 + b2


def _s5_scan(u, a_bar, b_bar, cm, h0, reverse):
    Bsz, L, G, GC = u.shape
    T = SSM_CHUNK
    n = L // T
    ub = u.reshape(Bsz, n, T, G, GC).transpose(1, 2, 0, 3, 4)
    a_seq = jnp.broadcast_to(a_bar, (T, 1) + a_bar.shape)
    first = T - 1 if reverse else 0

    def step(h, u_blk):
        bu = jnp.einsum('tbgc,gpc->tbgp', u_blk, b_bar)
        bu = bu.at[first].add(a_bar * h)
        _, hs = lax.associative_scan(_lin_combine, (a_seq, bu), axis=0, reverse=reverse)
        y = jnp.einsum('tbgp,gcp->tbgc', hs, cm).real
        return hs[T - 1 - first], y

    h_last, ys = lax.scan(step, h0, ub, reverse=reverse)
    return ys.transpose(2, 0, 1, 3, 4).reshape(Bsz, L, G, GC), h_last


def _s5_mixer(h, hc, a_re, a_im, log_dt, b_re, b_im, c_re, c_im, d_skip, w_glu, ctx_out):
    Bsz, L, D = h.shape
    f32 = jnp.float32
    lam = lax.complex(a_re.astype(f32), a_im.astype(f32))
    dt = jnp.exp(log_dt.astype(f32))[..., None]
    a_bar = jnp.exp(lam * dt)
    b = lax.complex(b_re.astype(f32), b_im.astype(f32))
    b_bar = ((a_bar - 1.0) / lam)[..., None] * b
    cm = lax.complex(c_re.astype(f32), c_im.astype(f32))
    u = h.astype(f32).reshape(Bsz, L, SSM_GROUPS, SSM_GROUP)
    uc = hc.astype(f32).reshape(Bsz, hc.shape[1], SSM_GROUPS, SSM_GROUP)
    h0 = jnp.zeros((Bsz, SSM_GROUPS, SSM_STATE), jnp.complex64)
    yc_f, sc_f = _s5_scan(uc, a_bar[0], b_bar[0], cm, h0, False)
    yc_b, sc_b = _s5_scan(uc, a_bar[1], b_bar[1], cm, h0, True)
    y_f, _ = _s5_scan(u, a_bar[0], b_bar[0], cm, sc_f, False)
    y_b, _ = _s5_scan(u, a_bar[1], b_bar[1], cm, sc_b, True)

    def finish(y, t):
        y = (y.reshape(Bsz, t.shape[1], D) + d_skip.astype(f32) * t.astype(f32)).astype(t.dtype)
        a, g = jnp.split(jax.nn.gelu(y, approximate=False) @ w_glu, 2, axis=-1)
        return a * jax.nn.sigmoid(g)

    y_lat = finish(y_f + y_b, h)
    y_ctx = finish(yc_f + yc_b, hc) if ctx_out else None
    return y_lat, y_ctx


def _diff_attention(h, hc, row, col, w_qkv, w_o, lam_p, subln_g, lam_init, ctx_out):
    Bsz, L, D = h.shape

    def project(t):
        q, k, v = jnp.split(t @ w_qkv, 3, axis=-1)
        n = t.shape[1]
        return (q.reshape(Bsz, n, DA_HEADS, 2, DA_DIM), k.reshape(Bsz, n, DA_HEADS, 2, DA_DIM),
                v.reshape(Bsz, n, DA_HEADS, 2 * DA_DIM))

    q, k, v = project(h)
    q, k = _rope_2d(q, row, col), _rope_2d(k, row, col)
    qc, kc, vc = project(hc)
    lp = lam_p.astype(jnp.float32)
    lam = jnp.exp(jnp.sum(lp[0] * lp[1])) - jnp.exp(jnp.sum(lp[2] * lp[3])) + lam_init
    keys = jnp.concatenate([kc, k], axis=1)
    vals = jnp.concatenate([vc, v], axis=1)
    scale = DA_DIM ** -0.5

    def attend(qb, kk, vv):
        s = jnp.einsum('bqhid,bkhid->bhiqk', qb, kk).astype(jnp.float32) * scale
        p = jax.nn.softmax(s, axis=-1)
        w = (p[:, :, 0] - lam * p[:, :, 1]).astype(vv.dtype)
        return jnp.einsum('bhqk,bkhe->bqhe', w, vv)

    nblk = L // Q_BLOCK
    qb = q.reshape(Bsz, nblk, Q_BLOCK, DA_HEADS, 2, DA_DIM).swapaxes(0, 1)
    o = lax.map(lambda blk: attend(blk, keys, vals), qb)
    o = o.swapaxes(0, 1).reshape(Bsz, L, DA_HEADS, 2 * DA_DIM)

    def out(o):
        return (_rmsnorm(o, subln_g) * (1.0 - lam_init)).reshape(Bsz, o.shape[1], D) @ w_o

    y_lat = out(o)
    y_ctx = out(attend(qc, kc, vc)) if ctx_out else None
    return y_lat, y_ctx


def _chunk_mlp_seq(t, w_in, ln_g, ln_b, w_s, b_s, w_out):
    Bsz, n, D = t.shape
    u, v = jnp.split(jax.nn.gelu(t @ w_in, approximate=False), 2, axis=-1)
    v = _layernorm(v, ln_g, ln_b)
    vg = v.reshape(Bsz, n // CM_CHUNK, CM_CHUNK, CM_GROUPS, CM_GW)
    vm = jnp.einsum('gst,bntgc->bnsgc', w_s, vg) + b_s.T[:, :, None]
    return (u * vm.reshape(Bsz, n, D)) @ w_out


def _chunk_mlp(h, hc, w_in, ln_g, ln_b, w_s, b_s, w_out, ctx_out):
    y_lat = _chunk_mlp_seq(h, w_in, ln_g, ln_b, w_s, b_s, w_out)
    y_ctx = _chunk_mlp_seq(hc, w_in, ln_g, ln_b, w_s, b_s, w_out) if ctx_out else None
    return y_lat, y_ctx


def _neighbourhood_attention(h, hc, w_qkv, w_o, rpb, ctx_out):
    Bsz, L, D = h.shape
    rows = L // GRID_W
    kh = min(WIN_H, rows)
    f32 = jnp.float32

    def project(t):
        q, k, v = jnp.split(t @ w_qkv, 3, axis=-1)
        sh = (Bsz, t.shape[1], NA_HEADS, NA_DIM)
        return q.reshape(sh), k.reshape(sh), v.reshape(sh)

    q, k, v = project(h)
    qc, kc, vc = project(hc)
    scale = NA_DIM ** -0.5
    grid = (Bsz, rows, GRID_W, NA_HEADS, NA_DIM)
    qg, kg, vg = q.reshape(grid), k.reshape(grid), v.reshape(grid)
    colq = jnp.arange(GRID_W)
    col_idx = jnp.clip(colq - WIN_W // 2, 0, GRID_W - WIN_W)[:, None] + jnp.arange(WIN_W)
    dc = col_idx - colq[:, None] + (WIN_W - 1)
    rpb_f = rpb.astype(f32)
    n_loc = kh * WIN_W

    def row_block(r):
        rs = jnp.clip(r - kh // 2, 0, rows - kh)
        k_sel = lax.dynamic_slice_in_dim(kg, rs, kh, axis=1)[:, :, col_idx]
        v_sel = lax.dynamic_slice_in_dim(vg, rs, kh, axis=1)[:, :, col_idx]
        q_r = lax.dynamic_index_in_dim(qg, r, axis=1, keepdims=False)
        dr = rs + jnp.arange(kh) - r + (WIN_H - 1)
        bias = rpb_f[:, dr[:, None, None], dc[None]].transpose(0, 2, 1, 3)
        s_loc = jnp.einsum('bqhd,biqjhd->bhqij', q_r, k_sel).astype(f32) * scale + bias
        s_ctx = jnp.einsum('bqhd,bkhd->bhqk', q_r, kc).astype(f32) * scale
        s = jnp.concatenate([s_loc.reshape(Bsz, NA_HEADS, GRID_W, n_loc), s_ctx], axis=-1)
        p = jax.nn.softmax(s, axis=-1).astype(v.dtype)
        p_loc = p[..., :n_loc].reshape(Bsz, NA_HEADS, GRID_W, kh, WIN_W)
        return (jnp.einsum('bhqij,biqjhd->bqhd', p_loc, v_sel)
                + jnp.einsum('bhqk,bkhd->bqhd', p[..., n_loc:], vc))

    o = lax.map(row_block, jnp.arange(rows))
    y_lat = o.swapaxes(0, 1).reshape(Bsz, L, D) @ w_o
    y_ctx = None
    if ctx_out:
        s = jnp.einsum('bqhd,bkhd->bhqk', qc, kc).astype(f32) * scale
        p = jax.nn.softmax(s, axis=-1).astype(vc.dtype)
        y_ctx = jnp.einsum('bhqk,bkhd->bqhd', p, vc).reshape(Bsz, hc.shape[1], D) @ w_o
    return y_lat, y_ctx


def setup_inputs(seed: int = 0) -> dict:
    key = jax.random.key(seed)
    keys = jax.random.split(key, 40)

    def nrm(i, shape, std):
        return jax.random.normal(keys[i], shape, jnp.float32) * std

    D, F, R = D_MODEL, FFN_HIDDEN, ADA_RANK
    G, P, GC = SSM_GROUPS, SSM_STATE, SSM_GROUP
    n_a, n_b, n_c, n_d = [len(range(m, DEPTH, N_MIXERS)) for m in range(N_MIXERS)]
    state_n = jnp.arange(P, dtype=jnp.float32)
    return {
        "x": nrm(0, (BATCH, SEQ, D), 1.0),
        "c": nrm(1, (BATCH, D), 1.0),
        "ctx": nrm(2, (BATCH, CTX_LEN, D), 1.0),
        "c_ctx": nrm(3, (D,), 1.0),
        "ada_down": nrm(4, (DEPTH, D, R), D ** -0.5),
        "ada_up": nrm(5, (DEPTH, R, 6 * D), 0.5 * R ** -0.5),
        "ada_b": nrm(6, (DEPTH, 6 * D), 0.01),
        "ln_g": 1.0 + nrm(7, (DEPTH, 2, D), 0.01),
        "ln_b": nrm(8, (DEPTH, 2, D), 0.01),
        "ffn_w_in": nrm(9, (DEPTH, D, 2 * F), D ** -0.5),
        "ffn_w_out": nrm(10, (DEPTH, F, D), BETA * F ** -0.5),
        "ssm_a_re": -0.5 + nrm(11, (n_a, 2, G, P), 0.01),
        "ssm_a_im": math.pi * state_n + nrm(12, (n_a, 2, G, P), 0.01),
        "ssm_log_dt": jax.random.uniform(keys[13], (n_a, 2, G), jnp.float32,
                                         math.log(DT_MIN), math.log(DT_MAX)),
        "ssm_b_re": nrm(14, (n_a, G, P, GC), (2 * GC) ** -0.5),
        "ssm_b_im": nrm(15, (n_a, G, P, GC), (2 * GC) ** -0.5),
        "ssm_c_re": nrm(16, (n_a, G, GC, P), (2 * P) ** -0.5),
        "ssm_c_im": nrm(17, (n_a, G, GC, P), (2 * P) ** -0.5),
        "ssm_d": nrm(18, (n_a, D), 1.0),
        "ssm_w_glu": nrm(19, (n_a, D, 2 * D), BETA * D ** -0.5),
        "da_w_qkv": nrm(20, (n_b, D, 3 * D), D ** -0.5),
        "da_w_o": nrm(21, (n_b, D, D), BETA * D ** -0.5),
        "da_lambda": nrm(22, (n_b, 4, DA_DIM), 0.1),
        "da_subln_g": 1.0 + nrm(23, (n_b, 2 * DA_DIM), 0.01),
        "cm_w_in": nrm(24, (n_c, D, 2 * D), D ** -0.5),
        "cm_ln_g": 1.0 + nrm(25, (n_c, D), 0.01),
        "cm_ln_b": nrm(26, (n_c, D), 0.01),
        "cm_w_s": nrm(27, (n_c, CM_GROUPS, CM_CHUNK, CM_CHUNK), CM_CHUNK ** -0.5),
        "cm_b_s": 1.0 + nrm(28, (n_c, CM_GROUPS, CM_CHUNK), 0.01),
        "cm_w_out": nrm(29, (n_c, D, D), BETA * D ** -0.5),
        "na_w_qkv": nrm(30, (n_d, D, 3 * D), D ** -0.5),
        "na_w_o": nrm(31, (n_d, D, D), BETA * D ** -0.5),
        "na_rpb": nrm(32, (n_d, NA_HEADS, 2 * WIN_H - 1, 2 * WIN_W - 1), 0.02),
    }


def reference(x, c, ctx, c_ctx, ada_down, ada_up, ada_b, ln_g, ln_b, ffn_w_in, ffn_w_out,
              ssm_a_re, ssm_a_im, ssm_log_dt, ssm_b_re, ssm_b_im, ssm_c_re, ssm_c_im, ssm_d,
              ssm_w_glu, da_w_qkv, da_w_o, da_lambda, da_subln_g,
              cm_w_in, cm_ln_g, cm_ln_b, cm_w_s, cm_b_s, cm_w_out,
              na_w_qkv, na_w_o, na_rpb):
    L = x.shape[1]
    t = jnp.arange(L)
    row, col = t // GRID_W, t % GRID_W
    xc = ctx
    for i in range(DEPTH):
        m, j = i % N_MIXERS, i // N_MIXERS
        ctx_out = i < DEPTH - 1
        sh1, sc1, g1, sh2, sc2, g2 = jnp.split(_adaln(c, ada_down[i], ada_up[i], ada_b[i]), 6, axis=-1)
        shc1, scc1, gc1, shc2, scc2, gc2 = jnp.split(
            _adaln(c_ctx[None], ada_down[i], ada_up[i], ada_b[i]), 6, axis=-1)
        h = x * (1.0 + sc1) + sh1
        hc = xc * (1.0 + scc1) + shc1
        if m == 0:
            y, yc = _s5_mixer(h, hc, ssm_a_re[j], ssm_a_im[j], ssm_log_dt[j], ssm_b_re[j],
                              ssm_b_im[j], ssm_c_re[j], ssm_c_im[j], ssm_d[j], ssm_w_glu[j], ctx_out)
        elif m == 1:
            lam_init = 0.8 - 0.6 * math.exp(-0.3 * i)
            y, yc = _diff_attention(h, hc, row, col, da_w_qkv[j], da_w_o[j], da_lambda[j],
                                    da_subln_g[j], lam_init, ctx_out)
        elif m == 2:
            y, yc = _chunk_mlp(h, hc, cm_w_in[j], cm_ln_g[j], cm_ln_b[j], cm_w_s[j], cm_b_s[j],
                               cm_w_out[j], ctx_out)
        else:
            y, yc = _neighbourhood_attention(h, hc, na_w_qkv[j], na_w_o[j], na_rpb[j], ctx_out)
        x = _layernorm(ALPHA * x + g1 * y, ln_g[i, 0], ln_b[i, 0])
        x = _layernorm(ALPHA * x + g2 * _swiglu(x * (1.0 + sc2) + sh2, ffn_w_in[i], ffn_w_out[i]),
                       ln_g[i, 1], ln_b[i, 1])
        if ctx_out:
            xc = _layernorm(ALPHA * xc + gc1 * yc, ln_g[i, 0], ln_b[i, 0])
            xc = _layernorm(ALPHA * xc + gc2 * _swiglu(xc * (1.0 + scc2) + shc2, ffn_w_in[i],
                                                          ffn_w_out[i]), ln_g[i, 1], ln_b[i, 1])
    return x
```

```python
import functools
import math
from typing import NamedTuple

import jax
import jax.numpy as jnp
from jax import lax
from jax.experimental import pallas as pl
from jax.experimental.pallas import tpu as pltpu

F32 = jnp.float32
BF16 = jnp.bfloat16

GRID_W = 64
LN_EPS = 1e-5
SSM_CHUNK = 16
ROPE_THETA = 10000.0
CM_CHUNK = 128
WIN_H = 8
WIN_W = 16
NEG_BIAS = -1e30

LANES = 128
ROW_TILE = 256
EW_ROW_TILE = 128
VMEM_LIMIT_MB = 56

_NT = (((1,), (1,)), ((), ()))


class _Rows(NamedTuple):
    batch: int
    ctx: int
    lat: int
    with_ctx: bool

    @property
    def seg(self):
        return (self.ctx if self.with_ctx else 0) + self.lat

    @property
    def total(self):
        return self.batch * self.seg

    def mod_row(self, i, tm):
        per = self.seg // tm
        b = i // per
        if not self.with_ctx:
            return b
        return jnp.where(i % per < self.ctx // tm, self.batch, b)

    def tile_in(self, other, i, tm):
        if other.with_ctx == self.with_ctx:
            return i
        assert other.with_ctx and not self.with_ctx
        return i + (i // (self.lat // tm) + 1) * (self.ctx // tm)


def _call(kernel, *, grid, in_specs, out_specs, out_shape, scratch_shapes=(), aliases=None, name=None):
    return pl.pallas_call(
        kernel, grid=grid, in_specs=in_specs, out_specs=out_specs, out_shape=out_shape,
        scratch_shapes=list(scratch_shapes), input_output_aliases=aliases or {}, name=name,
        compiler_params=pltpu.CompilerParams(
            dimension_semantics=("arbitrary",) * len(grid),
            vmem_limit_bytes=VMEM_LIMIT_MB << 20))


def _gelu(x):
    return 0.5 * x * (1.0 + lax.erf(x * (2.0 ** -0.5)))


def _adaln_kernel(c_ref, wd_ref, wu_ref, b_ref, o_ref):
    c = c_ref[...]
    s = c * jax.nn.sigmoid(c)
    t = jnp.dot(s, wd_ref[...], precision=lax.Precision.HIGHEST, preferred_element_type=F32)
    o_ref[...] = jnp.dot(t, wu_ref[...], precision=lax.Precision.HIGHEST,
                         preferred_element_type=F32) + b_ref[...]


def _adaln(cond, w_down, w_up, b):
    depth, d, r = w_down.shape
    n = w_up.shape[2]
    tn = d
    return _call(
        _adaln_kernel, grid=(depth, n // tn),
        in_specs=[pl.BlockSpec((8, d), lambda l, j: (0, 0)),
                  pl.BlockSpec((None, d, r), lambda l, j: (l, 0, 0)),
                  pl.BlockSpec((None, r, tn), lambda l, j: (l, 0, j)),
                  pl.BlockSpec((None, 1, tn), lambda l, j: (l, 0, j))],
        out_specs=pl.BlockSpec((None, 8, tn), lambda l, j: (l, 0, j)),
        out_shape=jax.ShapeDtypeStruct((depth, 8, n), F32), name="adaln",
    )(cond, w_down, w_up, b.reshape(depth, 1, n))


def _mod_spec(rows, tm, d, chunk):
    return pl.BlockSpec((None, 1, d), lambda i: (rows.mod_row(i, tm), 0, chunk))


def _modulate_kernel(x_ref, sh_ref, sc_ref, h_ref):
    h_ref[...] = (x_ref[...] * (1.0 + sc_ref[...]) + sh_ref[...]).astype(h_ref.dtype)


def _modulate(x, mods, rows, sh_chunk, sc_chunk):
    t, d = x.shape
    tm = EW_ROW_TILE
    return _call(
        _modulate_kernel, grid=(t // tm,),
        in_specs=[pl.BlockSpec((tm, d), lambda i: (i, 0)),
                  _mod_spec(rows, tm, d, sh_chunk), _mod_spec(rows, tm, d, sc_chunk)],
        out_specs=pl.BlockSpec((tm, d), lambda i: (i, 0)),
        out_shape=jax.ShapeDtypeStruct((t, d), BF16), name="modulate",
    )(x, mods, mods)


def _ln_res_kernel(*refs, alpha, eps, has_next):
    if has_next:
        x_ref, y_ref, gate_ref, g_ref, b_ref, sh_ref, sc_ref, xo_ref, ho_ref = refs
    else:
        x_ref, y_ref, gate_ref, g_ref, b_ref, xo_ref = refs
    z = alpha * x_ref[...] + gate_ref[...] * y_ref[...].astype(F32)
    mu = jnp.mean(z, axis=-1, keepdims=True)
    dz = z - mu
    var = jnp.mean(dz * dz, axis=-1, keepdims=True)
    xn = dz * lax.rsqrt(var + eps) * g_ref[...] + b_ref[...]
    xo_ref[...] = xn
    if has_next:
        ho_ref[...] = (xn * (1.0 + sc_ref[...]) + sh_ref[...]).astype(ho_ref.dtype)


def _ln_res(x, x_rows, y, rows, gate_mods, gate_chunk, ln_g, ln_b, alpha, next_mods=None, next_chunks=None):
    t, d = y.shape
    tm = EW_ROW_TILE
    has_next = next_mods is not None
    row_spec = pl.BlockSpec((tm, d), lambda i: (i, 0))
    vec_spec = pl.BlockSpec((1, d), lambda i: (0, 0))
    in_specs = [pl.BlockSpec((tm, d), lambda i: (rows.tile_in(x_rows, i, tm), 0)), row_spec,
                _mod_spec(rows, tm, d, gate_chunk), vec_spec, vec_spec]
    args = [x, y, gate_mods, ln_g.reshape(1, d), ln_b.reshape(1, d)]
    out_specs = [row_spec]
    out_shape = [jax.ShapeDtypeStruct((t, d), F32)]
    if has_next:
        in_specs += [_mod_spec(rows, tm, d, next_chunks[0]), _mod_spec(rows, tm, d, next_chunks[1])]
        args += [next_mods, next_mods]
        out_specs.append(row_spec)
        out_shape.append(jax.ShapeDtypeStruct((t, d), BF16))
    out = _call(
        functools.partial(_ln_res_kernel, alpha=alpha, eps=LN_EPS, has_next=has_next),
        grid=(t // tm,), in_specs=in_specs, out_specs=out_specs, out_shape=out_shape, name="ln_res",
    )(*args)
    return (out[0], out[1]) if has_next else (out[0], None)


def _swap32(a):
    lane = lax.broadcasted_iota(jnp.int32, a.shape, 1)
    return jnp.where(lane % 64 < 32, pltpu.roll(a, LANES - 32, axis=1), pltpu.roll(a, 32, axis=1))


def _mm_kernel(*refs, mode):
    x = refs[0][...]
    o_ref = refs[-1]
    a = jnp.dot(x, refs[1][...], preferred_element_type=F32)
    if mode == "plain":
        out = a
    elif mode == "gelu":
        out = _gelu(a)
    elif mode == "swiglu":
        b = jnp.dot(x, refs[2][...], preferred_element_type=F32)
        out = a * jax.nn.sigmoid(a) * b
    elif mode == "glu":
        b = jnp.dot(x, refs[2][...], preferred_element_type=F32)
        out = a * jax.nn.sigmoid(b)
    elif mode == "rope":
        cos = refs[2][...]
        sin = refs[3][...]
        pieces = []
        for g in range(a.shape[1] // LANES):
            ag = a[:, g * LANES:(g + 1) * LANES]
            pieces.append(ag * cos + _swap32(ag) * sin)
        out = jnp.concatenate(pieces, axis=1)
    else:
        raise ValueError(mode)
    o_ref[...] = out.astype(o_ref.dtype)


def _mm(x, w, n_out, tn, *, mode="plain", second_col=None, rope=None, out_dtype=BF16):
    t, k = x.shape
    tm = ROW_TILE
    assert t % tm == 0 and n_out % tn == 0
    in_specs = [pl.BlockSpec((tm, k), lambda j, i: (i, 0)),
                pl.BlockSpec((k, tn), lambda j, i: (0, j))]
    args = [x, w]
    if mode in ("swiglu", "glu"):
        off = second_col // tn
        in_specs.append(pl.BlockSpec((k, tn), lambda j, i: (0, j + off)))
        args.append(w)
    if mode == "rope":
        in_specs += [pl.BlockSpec((tm, LANES), lambda j, i: (i, 0))] * 2
        args += list(rope)
    return _call(
        functools.partial(_mm_kernel, mode=mode), grid=(n_out // tn, t // tm),
        in_specs=in_specs, out_specs=pl.BlockSpec((tm, tn), lambda j, i: (i, j)),
        out_shape=jax.ShapeDtypeStruct((t, n_out), out_dtype), name="mm_" + mode,
    )(*args)


def _ffn(h, w_in, w_out):
    f = w_out.shape[0]
    a = _mm(h, w_in.astype(BF16), f, 256, mode="swiglu", second_col=f)
    return _mm(a, w_out.astype(BF16), w_out.shape[1], 512)


def _cmul(ar, ai, br, bi):
    return ar * br - ai * bi, ar * bi + ai * br


def _s5_param_kernel(are_ref, aim_ref, ldt_ref, btr_ref, bti_ref, cre_ref, cim_ref,
                     kc_ref, bz_ref, coef_ref, *, q):
    cre = cre_ref[...]
    cim = cim_ref[...]
    gc, p = cre.shape
    w = q * gc
    lane = lax.broadcasted_iota(jnp.int32, (gc, w), 1)
    kmat = None
    cmt = []
    bz = []
    coef_rows = []
    for d in range(2):
        lr = are_ref[d]
        li = aim_ref[d]
        dt = jnp.exp(ldt_ref[d])
        ea = jnp.exp(lr * dt)
        ar = ea * jnp.cos(li * dt)
        ai = ea * jnp.sin(li * dt)
        den = lr * lr + li * li
        nr = ar - 1.0
        kr = (nr * lr + ai * li) / den
        ki = (ai * lr - nr * li) / den
        bbr, bbi = _cmul(kr, ki, btr_ref[...], bti_ref[...])
        pr = [jnp.ones_like(ar)]
        pi = [jnp.zeros_like(ar)]
        for _ in range(q):
            nr_, ni_ = _cmul(pr[-1], pi[-1], ar, ai)
            pr.append(nr_)
            pi.append(ni_)
        cp = [_cmul(cre, cim, pr[k], pi[k]) for k in range(q + 1)]
        bp = [_cmul(bbr, bbi, pr[k], pi[k]) for k in range(q)]
        bb = jnp.concatenate([bbr, -bbi], axis=1)
        if d == 0:
            vall = jnp.concatenate([jnp.concatenate(cp[l], axis=1) for l in range(q)], axis=0)
        else:
            vall = jnp.concatenate([jnp.concatenate(cp[q - 1 - l], axis=1) for l in range(q)], axis=0)
        mt = lax.dot_general(bb, vall, _NT, precision=lax.Precision.HIGHEST,
                             preferred_element_type=F32)
        blocks = []
        for s in range(q):
            if d == 0:
                n = gc * s
                blk = mt if n == 0 else jnp.where(lane >= n, pltpu.roll(mt, n, axis=1), 0.0)
            else:
                n = gc * (q - 1 - s)
                blk = mt if n == 0 else jnp.where(lane < w - n, pltpu.roll(mt, w - n, axis=1), 0.0)
            blocks.append(blk)
        kd = jnp.concatenate(blocks, axis=0)
        kmat = kd if kmat is None else kmat + kd
        zs = [bp[q - 1 - s] if d == 0 else bp[s] for s in range(q)]
        bz.append(jnp.concatenate([jnp.concatenate([zr, zi], axis=1) for zr, zi in zs], axis=0))
        bz.append(jnp.concatenate([jnp.concatenate([zi, zr], axis=1) for zr, zi in zs], axis=0))
        cs = [cp[t + 1] if d == 0 else cp[q - t] for t in range(q)]
        cmt.append(jnp.concatenate([jnp.concatenate([vr, -vi], axis=1) for vr, vi in cs], axis=0))
        coef_rows.append(jnp.concatenate([pr[q], pr[q]], axis=1))
        coef_rows.append(jnp.concatenate([-pi[q], pi[q]], axis=1))
    kc_ref[0:w, :] = kmat.astype(kc_ref.dtype)
    cm = jnp.concatenate(cmt, axis=1).T
    kc_ref[w:w + 4 * p, :] = cm.astype(kc_ref.dtype)
    bz_ref[...] = jnp.concatenate(bz, axis=1).astype(bz_ref.dtype)
    coef_rows += [jnp.zeros_like(coef_rows[0])] * 4
    coef_ref[...] = jnp.concatenate(coef_rows, axis=0)


def _s5_params(a_re, a_im, log_dt, b_re, b_im, c_re, c_im):
    _, g, p = a_re.shape
    gc = c_re.shape[1]
    q = SSM_CHUNK
    w = q * gc
    assert 2 * p == LANES
    dir_spec = pl.BlockSpec((2, None, 1, p), lambda i: (0, i, 0, 0))
    mat_spec = pl.BlockSpec((None, gc, p), lambda i: (i, 0, 0))
    ldt = jnp.broadcast_to(log_dt[:, :, None, None], (2, g, 1, p))
    return _call(
        functools.partial(_s5_param_kernel, q=q), grid=(g,),
        in_specs=[dir_spec, dir_spec, dir_spec, mat_spec, mat_spec, mat_spec, mat_spec],
        out_specs=[pl.BlockSpec((None, w + 4 * p, w), lambda i: (i, 0, 0)),
                   pl.BlockSpec((None, w, 8 * p), lambda i: (i, 0, 0)),
                   pl.BlockSpec((None, 8, 2 * p), lambda i: (i, 0, 0))],
        out_shape=[jax.ShapeDtypeStruct((g, w + 4 * p, w), BF16),
                   jax.ShapeDtypeStruct((g, w, 8 * p), BF16),
                   jax.ShapeDtypeStruct((g, 8, 2 * p), F32)], name="s5_params",
    )(a_re.reshape(2, g, 1, p), a_im.reshape(2, g, 1, p), ldt,
      jnp.swapaxes(b_re, 1, 2), jnp.swapaxes(b_im, 1, 2), c_re, c_im)


def _s5_scan_kernel(u_ref, kc_ref, bz_ref, coef_ref, y_ref, z_scr, s_scr, *, nb, nsub, nsub_ctx):
    u = u_ref[...]
    w = u.shape[1]
    z_scr[...] = jnp.dot(u, bz_ref[...], preferred_element_type=F32)
    p1 = (coef_ref[0:1, :], coef_ref[2:3, :])
    p2 = (coef_ref[1:2, :], coef_ref[3:4, :])

    sub = 8
    ntile, ntile_ctx = nsub // sub, nsub_ctx // sub

    def body(n, carry):
        tb = jnp.where(n < ntile_ctx, ntile_ctx - 1 - n, ntile + ntile_ctx - 1 - n)
        new = []
        for b in range(nb):
            for d, tile in ((0, n), (1, tb)):
                h, hs = carry[2 * (2 * b + d)], carry[2 * (2 * b + d) + 1]
                base = pl.multiple_of(b * nsub + tile * sub, sub)
                lo = 2 * LANES * d
                zt = z_scr[pl.ds(base, sub), lo:lo + LANES]
                zst = z_scr[pl.ds(base, sub), lo + LANES:lo + 2 * LANES]
                states = [None] * sub
                for i in (range(sub) if d == 0 else reversed(range(sub))):
                    states[i] = h
                    h, hs = (p1[d] * h + p2[d] * hs + zt[i:i + 1], p1[d] * hs - p2[d] * h + zst[i:i + 1])
                s_scr[pl.ds(base, sub), LANES * d:LANES * (d + 1)] = jnp.concatenate(states, axis=0)
                new += [h, hs]
        return tuple(new)

    zero = jnp.zeros((1, LANES), F32)
    lax.fori_loop(0, ntile, body, (zero,) * (4 * nb))
    y = jnp.dot(u, kc_ref[0:w, :], preferred_element_type=F32)
    y += jnp.dot(s_scr[...].astype(u.dtype), kc_ref[w:w + 2 * LANES, :], preferred_element_type=F32)
    y_ref[...] = y.astype(y_ref.dtype)


def _s5_finish_kernel(y_ref, h_ref, d_ref, o_ref):
    v = (y_ref[...].astype(F32) + d_ref[...] * h_ref[...].astype(F32)).astype(h_ref.dtype)
    o_ref[...] = _gelu(v.astype(F32)).astype(o_ref.dtype)


def _s5_mixer(h, rows, a_re, a_im, log_dt, b_re, b_im, c_re, c_im, d_skip, w_glu):
    t, d = h.shape
    g = a_re.shape[1]
    gc = d // g
    q = SSM_CHUNK
    w = q * gc
    r = t // q
    kc, bz, coef = _s5_params(a_re, a_im, log_dt, b_re, b_im, c_re, c_im)
    u = h.reshape(r, q, g, gc).transpose(2, 0, 1, 3).reshape(g, r, w)
    y = _call(
        functools.partial(_s5_scan_kernel, nb=rows.batch, nsub=rows.seg // q, nsub_ctx=rows.ctx // q),
        grid=(g,),
        in_specs=[pl.BlockSpec((None, r, w), lambda i: (i, 0, 0)),
                  pl.BlockSpec((None,) + kc.shape[1:], lambda i: (i, 0, 0)),
                  pl.BlockSpec((None,) + bz.shape[1:], lambda i: (i, 0, 0)),
                  pl.BlockSpec((None,) + coef.shape[1:], lambda i: (i, 0, 0))],
        out_specs=pl.BlockSpec((None, r, w), lambda i: (i, 0, 0)),
        out_shape=jax.ShapeDtypeStruct((g, r, w), BF16),
        scratch_shapes=[pltpu.VMEM((r, 4 * LANES), F32), pltpu.VMEM((r, 2 * LANES), F32)],
        name="s5_scan",
    )(u, kc, bz, coef)
    y = y.reshape(g, r, q, gc).transpose(1, 2, 0, 3).reshape(t, d)
    tm = EW_ROW_TILE
    row_spec = pl.BlockSpec((tm, d), lambda i: (i, 0))
    v = _call(
        _s5_finish_kernel, grid=(t // tm,),
        in_specs=[row_spec, row_spec, pl.BlockSpec((1, d), lambda i: (0, 0))],
        out_specs=row_spec, out_shape=jax.ShapeDtypeStruct((t, d), BF16), name="s5_finish",
    )(y, h, d_skip.reshape(1, d))
    return _mm(v, w_glu.astype(BF16), d, 256, mode="glu", second_col=d)


def _rope_tables(rows):
    half = LANES // 2
    quarter = half // 2
    tpos = jnp.arange(rows.lat)
    pos = jnp.stack([tpos // GRID_W, tpos % GRID_W], axis=1).astype(F32)
    inv = ROPE_THETA ** (-jnp.arange(quarter, dtype=F32) * 2.0 / half)
    ang = pos[:, :, None] * inv[None, None, :]
    ang = jnp.repeat(ang[:, :, None, :], 2, axis=2).reshape(rows.lat, LANES)
    sign = jnp.tile(jnp.repeat(jnp.array([-1.0, 1.0], F32), quarter), 2)
    cos = jnp.cos(ang)
    sin = jnp.sin(ang) * sign
    cos = jnp.concatenate([jnp.ones((rows.ctx, LANES), F32), cos], axis=0)
    sin = jnp.concatenate([jnp.zeros((rows.ctx, LANES), F32), sin], axis=0)
    return jnp.tile(cos, (rows.batch, 1)), jnp.tile(sin, (rows.batch, 1))


def _da_kernel(lam_ref, g_ref, q_ref, k_ref, v_ref, o_ref, *, scale, lam_init, eps):
    lp = lam_ref[...]
    dh = lp.shape[1]
    lam = (jnp.exp(jnp.sum(lp[0:1] * lp[1:2], axis=-1, keepdims=True))
           - jnp.exp(jnp.sum(lp[2:3] * lp[3:4], axis=-1, keepdims=True)) + lam_init)
    q = q_ref[...]
    k = k_ref[...]
    v = v_ref[...]
    probs = []
    for i in range(2):
        s = lax.dot_general(q[:, i * dh:(i + 1) * dh], k[:, i * dh:(i + 1) * dh], _NT,
                            preferred_element_type=F32) * scale
        e = jnp.exp(s - jnp.max(s, axis=-1, keepdims=True))
        probs.append(e * (1.0 / jnp.sum(e, axis=-1, keepdims=True)))
    wgt = (probs[0] - lam * probs[1]).astype(v.dtype)
    o = jnp.dot(wgt, v, preferred_element_type=F32)
    on = o * lax.rsqrt(jnp.mean(o * o, axis=-1, keepdims=True) + eps)
    o_ref[...] = (on * g_ref[...] * (1.0 - lam_init)).astype(o_ref.dtype)


def _diff_attention(h, rows, w_qkv, w_o, lam_p, subln_g, lam_init):
    t, d = h.shape
    dh = lam_p.shape[1]
    hw = 2 * dh
    heads = d // hw
    cos, sin = _rope_tables(rows)
    qk = _mm(h, w_qkv[:, :2 * d].astype(BF16), 2 * d, 512, mode="rope", rope=(cos, sin))
    v = _mm(h, w_qkv[:, 2 * d:].astype(BF16), d, 512)
    body = functools.partial(_da_kernel, scale=dh ** -0.5, lam_init=lam_init, eps=LN_EPS)
    lam_spec = pl.BlockSpec(lam_p.shape, lambda b, hd, i: (0, 0))
    g_spec = pl.BlockSpec((1, hw), lambda b, hd, i: (0, 0))
    tq = ROW_TILE
    s, c = rows.seg, rows.ctx
    o = _call(
        body, grid=(rows.batch, heads, rows.lat // tq),
        in_specs=[lam_spec, g_spec,
                  pl.BlockSpec((tq, hw), lambda b, hd, i: ((b * s + c) // tq + i, hd)),
                  pl.BlockSpec((s, hw), lambda b, hd, i: (b, heads + hd)),
                  pl.BlockSpec((s, hw), lambda b, hd, i: (b, hd))],
        out_specs=pl.BlockSpec((tq, hw), lambda b, hd, i: ((b * s + c) // tq + i, hd)),
        out_shape=jax.ShapeDtypeStruct((t, d), BF16), name="diff_attn_lat",
    )(lam_p, subln_g.reshape(1, hw), qk, qk, v)
    o = _call(
        lambda lam_ref, g_ref, q_ref, k_ref, v_ref, prev_ref, o_ref: body(lam_ref, g_ref, q_ref, k_ref, v_ref, o_ref),
        grid=(rows.batch, heads, 1),
        in_specs=[lam_spec, g_spec,
                  pl.BlockSpec((c, hw), lambda b, hd, i: (b * s // c, hd)),
                  pl.BlockSpec((c, hw), lambda b, hd, i: (b * s // c, heads + hd)),
                  pl.BlockSpec((c, hw), lambda b, hd, i: (b * s // c, hd)),
                  pl.BlockSpec(memory_space=pl.ANY)],
        out_specs=pl.BlockSpec((c, hw), lambda b, hd, i: (b * s // c, hd)),
        out_shape=jax.ShapeDtypeStruct((t, d), BF16), aliases={5: 0}, name="diff_attn_ctx",
    )(lam_p, subln_g.reshape(1, hw), qk, qk, v, o)
    return _mm(o, w_o.astype(BF16), d, 512)


def _cm_kernel(u_ref, v_ref, g_ref, b_ref, ws_ref, bs_ref, o_ref, *, eps):
    v = v_ref[...].astype(F32)
    mu = jnp.mean(v, axis=-1, keepdims=True)
    dv = v - mu
    var = jnp.mean(dv * dv, axis=-1, keepdims=True)
    vn = (dv * lax.rsqrt(var + eps) * g_ref[...] + b_ref[...]).astype(ws_ref.dtype)
    groups = ws_ref.shape[0]
    gw = vn.shape[1] // groups
    for g in range(groups):
        cols = slice(g * gw, (g + 1) * gw)
        vm = jnp.dot(ws_ref[g], vn[:, cols], preferred_element_type=F32) + bs_ref[:, g:g + 1]
        o_ref[:, cols] = (u_ref[:, cols].astype(F32) * vm).astype(o_ref.dtype)


def _chunk_mlp(h, w_in, ln_g, ln_b, w_s, b_s, w_out):
    t, d = h.shape
    uv = _mm(h, w_in.astype(BF16), 2 * d, 512, mode="gelu")
    tc = CM_CHUNK
    groups = w_s.shape[0]
    vec_spec = pl.BlockSpec((1, d), lambda i: (0, 0))
    gated = _call(
        functools.partial(_cm_kernel, eps=LN_EPS), grid=(t // tc,),
        in_specs=[pl.BlockSpec((tc, d), lambda i: (i, 0)), pl.BlockSpec((tc, d), lambda i: (i, 1)),
                  vec_spec, vec_spec,
                  pl.BlockSpec((groups, tc, tc), lambda i: (0, 0, 0)),
                  pl.BlockSpec((tc, groups), lambda i: (0, 0))],
        out_specs=pl.BlockSpec((tc, d), lambda i: (i, 0)),
        out_shape=jax.ShapeDtypeStruct((t, d), BF16), name="chunk_gate",
    )(uv, uv, ln_g.reshape(1, d), ln_b.reshape(1, d), w_s.astype(BF16), b_s.T)
    return _mm(gated, w_out.astype(BF16), d, 512)


def _na_bias_table(rpb, nrows):
    kh = min(WIN_H, nrows)
    i = jnp.arange(kh)
    dr = i[None, :] - jnp.arange(kh)[:, None] + (WIN_H - 1)
    cq = jnp.arange(GRID_W)
    cs = jnp.clip(cq - WIN_W // 2, 0, GRID_W - WIN_W)
    ck = jnp.arange(GRID_W)
    dc = jnp.clip(ck[None, :] - cq[:, None] + (WIN_W - 1), 0, 2 * WIN_W - 2)
    valid = (ck[None, :] >= cs[:, None]) & (ck[None, :] < cs[:, None] + WIN_W)
    tab = rpb.astype(F32)[:, dr[:, :, None, None], dc[None, None, :, :]]
    tab = jnp.where(valid[None, None, None], tab, NEG_BIAS)
    return tab.transpose(1, 0, 3, 2, 4).reshape(kh, rpb.shape[0], GRID_W, kh * GRID_W)


def _na_kernel(bm_ref, q_ref, k_ref, v_ref, o_ref, *, scale, dh, kh, nrows, ctx):
    r = pl.program_id(2)
    rs = jnp.clip(r - kh // 2, 0, nrows - kh)
    start = pl.multiple_of(ctx + rs * GRID_W, GRID_W)
    nloc = kh * GRID_W
    for hd in range(q_ref.shape[1] // dh):
        cols = slice(hd * dh, (hd + 1) * dh)
        qh = q_ref[:, cols]
        kl = k_ref[pl.ds(start, nloc), cols]
        vl = v_ref[pl.ds(start, nloc), cols]
        kc = k_ref[0:ctx, cols]
        vc = v_ref[0:ctx, cols]
        sl = lax.dot_general(qh, kl, _NT, preferred_element_type=F32) * scale + bm_ref[hd]
        sc = lax.dot_general(qh, kc, _NT, preferred_element_type=F32) * scale
        m = jnp.maximum(jnp.max(sl, axis=-1, keepdims=True), jnp.max(sc, axis=-1, keepdims=True))
        el = jnp.exp(sl - m)
        ec = jnp.exp(sc - m)
        den = jnp.sum(el, axis=-1, keepdims=True) + jnp.sum(ec, axis=-1, keepdims=True)
        o = (jnp.dot(el.astype(vl.dtype), vl, preferred_element_type=F32)
             + jnp.dot(ec.astype(vc.dtype), vc, preferred_element_type=F32))
        o_ref[:, cols] = (o * (1.0 / den)).astype(o_ref.dtype)


def _neighbourhood_attention(h, rows, w_qkv, w_o, rpb):
    t, d = h.shape
    heads = rpb.shape[0]
    dh = d // heads
    nrows = rows.lat // GRID_W
    kh = min(WIN_H, nrows)
    qkv = _mm(h, w_qkv.astype(BF16), 3 * d, 512)
    bm = _na_bias_table(rpb, nrows)
    hg = 4
    hw = hg * dh
    s, c = rows.seg, rows.ctx

    def off_idx(r):
        return jnp.clip(r - kh // 2, 0, nrows - kh)

    o = _call(
        functools.partial(_na_kernel, scale=dh ** -0.5, dh=dh, kh=kh, nrows=nrows, ctx=c),
        grid=(rows.batch, heads // hg, nrows),
        in_specs=[pl.BlockSpec((None, hg, GRID_W, kh * GRID_W), lambda b, g, r: (r - off_idx(r), g, 0, 0)),
                  pl.BlockSpec((GRID_W, hw), lambda b, g, r: ((b * s + c) // GRID_W + r, g)),
                  pl.BlockSpec((s, hw), lambda b, g, r: (b, d // hw + g)),
                  pl.BlockSpec((s, hw), lambda b, g, r: (b, 2 * d // hw + g))],
        out_specs=pl.BlockSpec((GRID_W, hw), lambda b, g, r: (b * nrows + r, g)),
        out_shape=jax.ShapeDtypeStruct((rows.batch * rows.lat, d), BF16), name="nbr_attn",
    )(bm, qkv, qkv, qkv)
    return _mm(o, w_o.astype(BF16), d, 512)


def kernel(x, c, ctx, c_ctx, ada_down, ada_up, ada_b, ln_g, ln_b, ffn_w_in, ffn_w_out,
           ssm_a_re, ssm_a_im, ssm_log_dt, ssm_b_re, ssm_b_im, ssm_c_re, ssm_c_im, ssm_d,
           ssm_w_glu, da_w_qkv, da_w_o, da_lambda, da_subln_g,
           cm_w_in, cm_ln_g, cm_ln_b, cm_w_s, cm_b_s, cm_w_out,
           na_w_qkv, na_w_o, na_rpb):
    bsz, seq, d = x.shape
    nctx = ctx.shape[1]
    depth = ada_down.shape[0]
    assert depth == 4 and bsz + 1 <= 8, "one layer per mixer kind; the last layer drops the context stream"
    assert nctx % ROW_TILE == 0 and seq % ROW_TILE == 0 and seq % GRID_W == 0
    alpha = (2 * depth) ** 0.25

    cond = jnp.zeros((8, d), F32).at[:bsz].set(c).at[bsz].set(c_ctx)
    mods = _adaln(cond, ada_down, ada_up, ada_b).reshape(depth, 8, 1, 6 * d)
    uni = _Rows(bsz, nctx, seq, True)
    lat = _Rows(bsz, nctx, seq, False)

    xs = jnp.concatenate([ctx, x], axis=1).reshape(uni.total, d)
    h = _modulate(xs, mods[0], uni, 0, 1)
    for i in range(depth):
        rows = uni
        if i == 0:
            y = _s5_mixer(h, uni, ssm_a_re[0], ssm_a_im[0], ssm_log_dt[0], ssm_b_re[0], ssm_b_im[0],
                          ssm_c_re[0], ssm_c_im[0], ssm_d[0], ssm_w_glu[0])
        elif i == 1:
            y = _diff_attention(h, uni, da_w_qkv[0], da_w_o[0], da_lambda[0], da_subln_g[0],
                                0.8 - 0.6 * math.exp(-0.3 * i))
        elif i == 2:
            y = _chunk_mlp(h, cm_w_in[0], cm_ln_g[0], cm_ln_b[0], cm_w_s[0], cm_b_s[0], cm_w_out[0])
        else:
            y = _neighbourhood_attention(h, uni, na_w_qkv[0], na_w_o[0], na_rpb[0])
            rows = lat
        xs, h = _ln_res(xs, uni, y, rows, mods[i], 2, ln_g[i, 0], ln_b[i, 0], alpha,
                        next_mods=mods[i], next_chunks=(3, 4))
        y = _ffn(h, ffn_w_in[i], ffn_w_out[i])
        if i + 1 < depth:
            xs, h = _ln_res(xs, rows, y, rows, mods[i], 5, ln_g[i, 1], ln_b[i, 1], alpha,
                            next_mods=mods[i + 1], next_chunks=(0, 1))
        else:
            xs, _ = _ln_res(xs, rows, y, rows, mods[i], 5, ln_g[i, 1], ln_b[i, 1], alpha)
    return xs.reshape(bsz, seq, d)
```

```python
import functools
import math
from typing import NamedTuple

import jax
import jax.numpy as jnp
from jax import lax
from jax.experimental import pallas as pl
from jax.experimental.pallas import tpu as pltpu

F32 = jnp.float32
BF16 = jnp.bfloat16

GRID_W = 64
LN_EPS = 1e-5
SSM_CHUNK = 16
ROPE_THETA = 10000.0
CM_CHUNK = 128
WIN_H = 8
WIN_W = 16
NEG_BIAS = -1e30
LOG2E = 1.4426950408889634

LANES = 128
ROW_TILE = 256
EW_ROW_TILE = 128
VMEM_LIMIT_MB = 56

_NT = (((1,), (1,)), ((), ()))


class _Rows(NamedTuple):
    batch: int
    ctx: int
    lat: int
    with_ctx: bool

    @property
    def seg(self):
        return (self.ctx if self.with_ctx else 0) + self.lat

    @property
    def total(self):
        return self.batch * self.seg

    def mod_row(self, i, tm):
        per = self.seg // tm
        b = i // per
        if not self.with_ctx:
            return b
        return jnp.where(i % per < self.ctx // tm, self.batch, b)

    def tile_in(self, other, i, tm):
        if other.with_ctx == self.with_ctx:
            return i
        assert other.with_ctx and not self.with_ctx
        return i + (i // (self.lat // tm) + 1) * (self.ctx // tm)


def _call(kernel, *, grid, in_specs, out_specs, out_shape, scratch_shapes=(), aliases=None, name=None):
    return pl.pallas_call(
        kernel, grid=grid, in_specs=in_specs, out_specs=out_specs, out_shape=out_shape,
        scratch_shapes=list(scratch_shapes), input_output_aliases=aliases or {}, name=name,
        compiler_params=pltpu.CompilerParams(
            dimension_semantics=("arbitrary",) * len(grid),
            vmem_limit_bytes=VMEM_LIMIT_MB << 20))


def _gelu(x):
    return 0.5 * x * (1.0 + lax.erf(x * (2.0 ** -0.5)))


def _adaln_kernel(c_ref, wd_ref, wu_ref, b_ref, o_ref):
    c = c_ref[...]
    s = c * jax.nn.sigmoid(c)
    t = jnp.dot(s, wd_ref[...], precision=lax.Precision.HIGHEST, preferred_element_type=F32)
    o_ref[...] = jnp.dot(t, wu_ref[...], precision=lax.Precision.HIGHEST,
                         preferred_element_type=F32) + b_ref[...]


def _adaln(cond, w_down, w_up, b):
    depth, d, r = w_down.shape
    n = w_up.shape[2]
    tn = d
    return _call(
        _adaln_kernel, grid=(depth, n // tn),
        in_specs=[pl.BlockSpec((8, d), lambda l, j: (0, 0)),
                  pl.BlockSpec((None, d, r), lambda l, j: (l, 0, 0)),
                  pl.BlockSpec((None, r, tn), lambda l, j: (l, 0, j)),
                  pl.BlockSpec((None, 1, tn), lambda l, j: (l, 0, j))],
        out_specs=pl.BlockSpec((None, 8, tn), lambda l, j: (l, 0, j)),
        out_shape=jax.ShapeDtypeStruct((depth, 8, n), F32), name="adaln",
    )(cond, w_down, w_up, b.reshape(depth, 1, n))


def _mod_spec(rows, tm, d, chunk):
    return pl.BlockSpec((None, 1, d), lambda i: (rows.mod_row(i, tm), 0, chunk))


def _modulate_kernel(x_ref, sh_ref, sc_ref, h_ref):
    h_ref[...] = (x_ref[...] * (1.0 + sc_ref[...]) + sh_ref[...]).astype(h_ref.dtype)


def _modulate(x, mods, rows, sh_chunk, sc_chunk):
    t, d = x.shape
    tm = EW_ROW_TILE
    return _call(
        _modulate_kernel, grid=(t // tm,),
        in_specs=[pl.BlockSpec((tm, d), lambda i: (i, 0)),
                  _mod_spec(rows, tm, d, sh_chunk), _mod_spec(rows, tm, d, sc_chunk)],
        out_specs=pl.BlockSpec((tm, d), lambda i: (i, 0)),
        out_shape=jax.ShapeDtypeStruct((t, d), BF16), name="modulate",
    )(x, mods, mods)


def _ln_res_kernel(*refs, alpha, eps, has_next):
    if has_next:
        x_ref, y_ref, gate_ref, g_ref, b_ref, sh_ref, sc_ref, xo_ref, ho_ref = refs
    else:
        x_ref, y_ref, gate_ref, g_ref, b_ref, xo_ref = refs
    z = alpha * x_ref[...] + gate_ref[...] * y_ref[...].astype(F32)
    mu = jnp.mean(z, axis=-1, keepdims=True)
    dz = z - mu
    var = jnp.mean(dz * dz, axis=-1, keepdims=True)
    xn = dz * lax.rsqrt(var + eps) * g_ref[...] + b_ref[...]
    xo_ref[...] = xn
    if has_next:
        ho_ref[...] = (xn * (1.0 + sc_ref[...]) + sh_ref[...]).astype(ho_ref.dtype)


def _ln_res(x, x_rows, y, rows, gate_mods, gate_chunk, ln_g, ln_b, alpha, next_mods=None, next_chunks=None):
    t, d = y.shape
    tm = EW_ROW_TILE
    has_next = next_mods is not None
    row_spec = pl.BlockSpec((tm, d), lambda i: (i, 0))
    vec_spec = pl.BlockSpec((1, d), lambda i: (0, 0))
    in_specs = [pl.BlockSpec((tm, d), lambda i: (rows.tile_in(x_rows, i, tm), 0)), row_spec,
                _mod_spec(rows, tm, d, gate_chunk), vec_spec, vec_spec]
    args = [x, y, gate_mods, ln_g.reshape(1, d), ln_b.reshape(1, d)]
    out_specs = [row_spec]
    out_shape = [jax.ShapeDtypeStruct((t, d), F32)]
    if has_next:
        in_specs += [_mod_spec(rows, tm, d, next_chunks[0]), _mod_spec(rows, tm, d, next_chunks[1])]
        args += [next_mods, next_mods]
        out_specs.append(row_spec)
        out_shape.append(jax.ShapeDtypeStruct((t, d), BF16))
    out = _call(
        functools.partial(_ln_res_kernel, alpha=alpha, eps=LN_EPS, has_next=has_next),
        grid=(t // tm,), in_specs=in_specs, out_specs=out_specs, out_shape=out_shape, name="ln_res",
    )(*args)
    return (out[0], out[1]) if has_next else (out[0], None)


def _swap32(a):
    lane = lax.broadcasted_iota(jnp.int32, a.shape, 1)
    return jnp.where(lane % 64 < 32, pltpu.roll(a, LANES - 32, axis=1), pltpu.roll(a, 32, axis=1))


def _mm_kernel(*refs, mode, nw, cast, q_tiles, q_scale):
    x = refs[0][...]
    w_refs = refs[1:1 + nw]
    o_ref = refs[-1 - (nw if cast else 0)]
    if cast:
        scr = refs[len(refs) - nw:]

        @pl.when(pl.program_id(1) == 0)
        def _():
            for w_ref, s_ref in zip(w_refs, scr):
                s_ref[...] = w_ref[...].astype(s_ref.dtype)

        w_refs = scr
    a = jnp.dot(x, w_refs[0][...], preferred_element_type=F32)
    if mode == "plain":
        out = a
    elif mode == "gelu":
        out = _gelu(a)
    elif mode == "swiglu":
        b = jnp.dot(x, w_refs[1][...], preferred_element_type=F32)
        out = a * jax.nn.sigmoid(a) * b
    elif mode == "glu":
        b = jnp.dot(x, w_refs[1][...], preferred_element_type=F32)
        out = a * jax.nn.sigmoid(b)
    elif mode == "rope":
        cos = refs[1 + nw][...]
        sin = refs[2 + nw][...]
        pieces = []
        for g in range(a.shape[1] // LANES):
            ag = a[:, g * LANES:(g + 1) * LANES]
            pieces.append(ag * cos + _swap32(ag) * sin)
        out = jnp.concatenate(pieces, axis=1)
    else:
        raise ValueError(mode)
    if q_tiles:
        out = out * jnp.where(pl.program_id(0) < q_tiles, q_scale, 1.0)
    o_ref[...] = out.astype(o_ref.dtype)


def _row_tile(t, k):
    budget = (24 << 20) if k <= 8192 else (12 << 20)
    for nt in range(1, t // 16 + 1):
        tm = t // nt
        if t % nt == 0 and tm % 16 == 0 and 4 * tm * k <= budget:
            return tm
    raise ValueError((t, k))


def _mm(x, w, n_out, tn, *, mode="plain", col0=0, second_col=None, rope=None, q_cols=0, q_scale=1.0,
        out_dtype=BF16):
    t, k = x.shape
    tm = _row_tile(t, k)
    assert n_out % tn == 0 and col0 % tn == 0 and q_cols % tn == 0
    cast = w.dtype != BF16
    offs = [col0 // tn] + ([second_col // tn] if mode in ("swiglu", "glu") else [])
    in_specs = [pl.BlockSpec((tm, k), lambda j, i: (i, 0))]
    in_specs += [pl.BlockSpec((k, tn), lambda j, i, off=off: (0, j + off)) for off in offs]
    args = [x] + [w] * len(offs)
    if mode == "rope":
        in_specs += [pl.BlockSpec((tm, LANES), lambda j, i: (i, 0))] * 2
        args += list(rope)
    return _call(
        functools.partial(_mm_kernel, mode=mode, nw=len(offs), cast=cast, q_tiles=q_cols // tn, q_scale=q_scale),
        grid=(n_out // tn, t // tm),
        in_specs=in_specs, out_specs=pl.BlockSpec((tm, tn), lambda j, i: (i, j)),
        out_shape=jax.ShapeDtypeStruct((t, n_out), out_dtype),
        scratch_shapes=[pltpu.VMEM((k, tn), BF16)] * len(offs) if cast else (), name="mm_" + mode,
    )(*args)


def _ffn(h, w_in, w_out):
    f = w_out.shape[0]
    a = _mm(h, w_in, f, 256, mode="swiglu", second_col=f)
    return _mm(a, w_out.astype(BF16), w_out.shape[1], 512)


def _cmul(ar, ai, br, bi):
    return ar * br - ai * bi, ar * bi + ai * br


def _s5_param_kernel(are_ref, aim_ref, ldt_ref, btr_ref, bti_ref, cre_ref, cim_ref,
                     kc_ref, bz_ref, coef_ref, *, q):
    cre = cre_ref[...]
    cim = cim_ref[...]
    gc, p = cre.shape
    w = q * gc
    lane = lax.broadcasted_iota(jnp.int32, (gc, w), 1)
    kmat = None
    cmt = []
    bz = []
    coef_rows = []
    for d in range(2):
        lr = are_ref[d]
        li = aim_ref[d]
        dt = jnp.exp(ldt_ref[d])
        ea = jnp.exp(lr * dt)
        ar = ea * jnp.cos(li * dt)
        ai = ea * jnp.sin(li * dt)
        den = lr * lr + li * li
        nr = ar - 1.0
        kr = (nr * lr + ai * li) / den
        ki = (ai * lr - nr * li) / den
        bbr, bbi = _cmul(kr, ki, btr_ref[...], bti_ref[...])
        pr = [jnp.ones_like(ar)]
        pi = [jnp.zeros_like(ar)]
        for _ in range(q):
            nr_, ni_ = _cmul(pr[-1], pi[-1], ar, ai)
            pr.append(nr_)
            pi.append(ni_)
        cp = [_cmul(cre, cim, pr[k], pi[k]) for k in range(q + 1)]
        bp = [_cmul(bbr, bbi, pr[k], pi[k]) for k in range(q)]
        bb = jnp.concatenate([bbr, -bbi], axis=1)
        if d == 0:
            vall = jnp.concatenate([jnp.concatenate(cp[l], axis=1) for l in range(q)], axis=0)
        else:
            vall = jnp.concatenate([jnp.concatenate(cp[q - 1 - l], axis=1) for l in range(q)], axis=0)
        mt = lax.dot_general(bb, vall, _NT, precision=lax.Precision.HIGHEST,
                             preferred_element_type=F32)
        blocks = []
        for s in range(q):
            if d == 0:
                n = gc * s
                blk = mt if n == 0 else jnp.where(lane >= n, pltpu.roll(mt, n, axis=1), 0.0)
            else:
                n = gc * (q - 1 - s)
                blk = mt if n == 0 else jnp.where(lane < w - n, pltpu.roll(mt, w - n, axis=1), 0.0)
            blocks.append(blk)
        kd = jnp.concatenate(blocks, axis=0)
        kmat = kd if kmat is None else kmat + kd
        zs = [bp[q - 1 - s] if d == 0 else bp[s] for s in range(q)]
        bz.append(jnp.concatenate([jnp.concatenate([zr, zi], axis=1) for zr, zi in zs], axis=0))
        bz.append(jnp.concatenate([jnp.concatenate([zi, zr], axis=1) for zr, zi in zs], axis=0))
        cs = [cp[t + 1] if d == 0 else cp[q - t] for t in range(q)]
        cmt.append(jnp.concatenate([jnp.concatenate([vr, -vi], axis=1) for vr, vi in cs], axis=0))
        coef_rows.append(jnp.concatenate([pr[q], pr[q]], axis=1))
        coef_rows.append(jnp.concatenate([-pi[q], pi[q]], axis=1))
    kc_ref[0:w, :] = kmat.astype(kc_ref.dtype)
    cm = jnp.concatenate(cmt, axis=1).T
    kc_ref[w:w + 4 * p, :] = cm.astype(kc_ref.dtype)
    bz_ref[...] = jnp.concatenate(bz, axis=1).astype(bz_ref.dtype)
    coef_rows += [jnp.zeros_like(coef_rows[0])] * 4
    coef_ref[...] = jnp.concatenate(coef_rows, axis=0)


def _s5_params(a_re, a_im, log_dt, b_re, b_im, c_re, c_im):
    _, g, p = a_re.shape
    gc = c_re.shape[1]
    q = SSM_CHUNK
    w = q * gc
    assert 2 * p == LANES
    dir_spec = pl.BlockSpec((2, None, 1, p), lambda i: (0, i, 0, 0))
    mat_spec = pl.BlockSpec((None, gc, p), lambda i: (i, 0, 0))
    ldt = jnp.broadcast_to(log_dt[:, :, None, None], (2, g, 1, p))
    return _call(
        functools.partial(_s5_param_kernel, q=q), grid=(g,),
        in_specs=[dir_spec, dir_spec, dir_spec, mat_spec, mat_spec, mat_spec, mat_spec],
        out_specs=[pl.BlockSpec((None, w + 4 * p, w), lambda i: (i, 0, 0)),
                   pl.BlockSpec((None, w, 8 * p), lambda i: (i, 0, 0)),
                   pl.BlockSpec((None, 8, 2 * p), lambda i: (i, 0, 0))],
        out_shape=[jax.ShapeDtypeStruct((g, w + 4 * p, w), BF16),
                   jax.ShapeDtypeStruct((g, w, 8 * p), BF16),
                   jax.ShapeDtypeStruct((g, 8, 2 * p), F32)], name="s5_params",
    )(a_re.reshape(2, g, 1, p), a_im.reshape(2, g, 1, p), ldt,
      jnp.swapaxes(b_re, 1, 2), jnp.swapaxes(b_im, 1, 2), c_re, c_im)


def _s5_scan_kernel(u_ref, kc_ref, bz_ref, coef_ref, y_ref, z_scr, s_scr, *, nb, nsub, nsub_ctx):
    u = u_ref[...]
    w = u.shape[1]
    z_scr[...] = jnp.dot(u, bz_ref[...], preferred_element_type=F32)
    p1 = (coef_ref[0:1, :], coef_ref[2:3, :])
    p2 = (coef_ref[1:2, :], coef_ref[3:4, :])

    sub = 8
    ntile, ntile_ctx = nsub // sub, nsub_ctx // sub

    def body(n, carry):
        tb = jnp.where(n < ntile_ctx, ntile_ctx - 1 - n, ntile + ntile_ctx - 1 - n)
        new = []
        for b in range(nb):
            for d, tile in ((0, n), (1, tb)):
                h, hs = carry[2 * (2 * b + d)], carry[2 * (2 * b + d) + 1]
                base = pl.multiple_of(b * nsub + tile * sub, sub)
                lo = 2 * LANES * d
                zt = z_scr[pl.ds(base, sub), lo:lo + LANES]
                zst = z_scr[pl.ds(base, sub), lo + LANES:lo + 2 * LANES]
                states = [None] * sub
                for i in (range(sub) if d == 0 else reversed(range(sub))):
                    states[i] = h
                    h, hs = (p1[d] * h + p2[d] * hs + zt[i:i + 1], p1[d] * hs - p2[d] * h + zst[i:i + 1])
                s_scr[pl.ds(base, sub), LANES * d:LANES * (d + 1)] = jnp.concatenate(states, axis=0)
                new += [h, hs]
        return tuple(new)

    zero = jnp.zeros((1, LANES), F32)
    lax.fori_loop(0, ntile, body, (zero,) * (4 * nb))
    y = jnp.dot(u, kc_ref[0:w, :], preferred_element_type=F32)
    y += jnp.dot(s_scr[...].astype(u.dtype), kc_ref[w:w + 2 * LANES, :], preferred_element_type=F32)
    y_ref[...] = y.astype(y_ref.dtype)


def _s5_finish_kernel(y_ref, h_ref, d_ref, o_ref):
    v = (y_ref[...].astype(F32) + d_ref[...] * h_ref[...].astype(F32)).astype(h_ref.dtype)
    o_ref[...] = _gelu(v.astype(F32)).astype(o_ref.dtype)


def _s5_mixer(h, rows, a_re, a_im, log_dt, b_re, b_im, c_re, c_im, d_skip, w_glu):
    t, d = h.shape
    g = a_re.shape[1]
    gc = d // g
    q = SSM_CHUNK
    w = q * gc
    r = t // q
    kc, bz, coef = _s5_params(a_re, a_im, log_dt, b_re, b_im, c_re, c_im)
    u = h.reshape(r, q, g, gc).transpose(2, 0, 1, 3).reshape(g, r, w)
    y = _call(
        functools.partial(_s5_scan_kernel, nb=rows.batch, nsub=rows.seg // q, nsub_ctx=rows.ctx // q),
        grid=(g,),
        in_specs=[pl.BlockSpec((None, r, w), lambda i: (i, 0, 0)),
                  pl.BlockSpec((None,) + kc.shape[1:], lambda i: (i, 0, 0)),
                  pl.BlockSpec((None,) + bz.shape[1:], lambda i: (i, 0, 0)),
                  pl.BlockSpec((None,) + coef.shape[1:], lambda i: (i, 0, 0))],
        out_specs=pl.BlockSpec((None, r, w), lambda i: (i, 0, 0)),
        out_shape=jax.ShapeDtypeStruct((g, r, w), BF16),
        scratch_shapes=[pltpu.VMEM((r, 4 * LANES), F32), pltpu.VMEM((r, 2 * LANES), F32)],
        name="s5_scan",
    )(u, kc, bz, coef)
    y = y.reshape(g, r, q, gc).transpose(1, 2, 0, 3).reshape(t, d)
    tm = EW_ROW_TILE
    row_spec = pl.BlockSpec((tm, d), lambda i: (i, 0))
    v = _call(
        _s5_finish_kernel, grid=(t // tm,),
        in_specs=[row_spec, row_spec, pl.BlockSpec((1, d), lambda i: (0, 0))],
        out_specs=row_spec, out_shape=jax.ShapeDtypeStruct((t, d), BF16), name="s5_finish",
    )(y, h, d_skip.reshape(1, d))
    return _mm(v, w_glu, d, 256, mode="glu", second_col=d)


def _rope_tables(rows):
    half = LANES // 2
    quarter = half // 2
    tpos = jnp.arange(rows.lat)
    pos = jnp.stack([tpos // GRID_W, tpos % GRID_W], axis=1).astype(F32)
    inv = ROPE_THETA ** (-jnp.arange(quarter, dtype=F32) * 2.0 / half)
    ang = pos[:, :, None] * inv[None, None, :]
    ang = jnp.repeat(ang[:, :, None, :], 2, axis=2).reshape(rows.lat, LANES)
    sign = jnp.tile(jnp.repeat(jnp.array([-1.0, 1.0], F32), quarter), 2)
    cos = jnp.cos(ang)
    sin = jnp.sin(ang) * sign
    cos = jnp.concatenate([jnp.ones((rows.ctx, LANES), F32), cos], axis=0)
    sin = jnp.concatenate([jnp.zeros((rows.ctx, LANES), F32), sin], axis=0)
    return jnp.tile(cos, (rows.batch, 1)), jnp.tile(sin, (rows.batch, 1))


def _da_kernel(lam_ref, g_ref, q_ref, k_ref, v_ref, o_ref, *, lam_init, eps):
    lp = lam_ref[...]
    dh = lp.shape[1]
    lam = (jnp.exp(jnp.sum(lp[0:1] * lp[1:2], axis=-1, keepdims=True))
           - jnp.exp(jnp.sum(lp[2:3] * lp[3:4], axis=-1, keepdims=True)) + lam_init)
    q = q_ref[...]
    k = k_ref[...]
    v = v_ref[...]
    outs = []
    for i in range(2):
        s = lax.dot_general(q[:, i * dh:(i + 1) * dh], k[:, i * dh:(i + 1) * dh], _NT,
                            preferred_element_type=F32)
        e = jnp.exp2(s - jnp.max(s, axis=-1, keepdims=True))
        den = jnp.sum(e, axis=-1, keepdims=True)
        outs.append(jnp.dot(e.astype(v.dtype), v, preferred_element_type=F32) * (1.0 / den))
    o = outs[0] - lam * outs[1]
    on = o * lax.rsqrt(jnp.mean(o * o, axis=-1, keepdims=True) + eps)
    o_ref[...] = (on * g_ref[...] * (1.0 - lam_init)).astype(o_ref.dtype)


def _diff_attention(h, rows, w_qkv, w_o, lam_p, subln_g, lam_init):
    t, d = h.shape
    dh = lam_p.shape[1]
    hw = 2 * dh
    heads = d // hw
    cos, sin = _rope_tables(rows)
    qk = _mm(h, w_qkv, 2 * d, 512, mode="rope", rope=(cos, sin), q_cols=d, q_scale=dh ** -0.5 * LOG2E)
    v = _mm(h, w_qkv, d, 512, col0=2 * d)
    body = functools.partial(_da_kernel, lam_init=lam_init, eps=LN_EPS)
    lam_spec = pl.BlockSpec(lam_p.shape, lambda b, hd, i: (0, 0))
    g_spec = pl.BlockSpec((1, hw), lambda b, hd, i: (0, 0))
    tq = ROW_TILE
    s, c = rows.seg, rows.ctx
    o = _call(
        body, grid=(rows.batch, heads, rows.lat // tq),
        in_specs=[lam_spec, g_spec,
                  pl.BlockSpec((tq, hw), lambda b, hd, i: ((b * s + c) // tq + i, hd)),
                  pl.BlockSpec((s, hw), lambda b, hd, i: (b, heads + hd)),
                  pl.BlockSpec((s, hw), lambda b, hd, i: (b, hd))],
        out_specs=pl.BlockSpec((tq, hw), lambda b, hd, i: ((b * s + c) // tq + i, hd)),
        out_shape=jax.ShapeDtypeStruct((t, d), BF16), name="diff_attn_lat",
    )(lam_p, subln_g.reshape(1, hw), qk, qk, v)
    o = _call(
        lambda lam_ref, g_ref, q_ref, k_ref, v_ref, prev_ref, o_ref: body(lam_ref, g_ref, q_ref, k_ref, v_ref, o_ref),
        grid=(rows.batch, heads, 1),
        in_specs=[lam_spec, g_spec,
                  pl.BlockSpec((c, hw), lambda b, hd, i: (b * s // c, hd)),
                  pl.BlockSpec((c, hw), lambda b, hd, i: (b * s // c, heads + hd)),
                  pl.BlockSpec((c, hw), lambda b, hd, i: (b * s // c, hd)),
                  pl.BlockSpec(memory_space=pl.ANY)],
        out_specs=pl.BlockSpec((c, hw), lambda b, hd, i: (b * s // c, hd)),
        out_shape=jax.ShapeDtypeStruct((t, d), BF16), aliases={5: 0}, name="diff_attn_ctx",
    )(lam_p, subln_g.reshape(1, hw), qk, qk, v, o)
    return _mm(o, w_o, d, 512)


def _cm_kernel(u_ref, v_ref, g_ref, b_ref, ws_ref, bs_ref, o_ref, *, eps):
    v = v_ref[...].astype(F32)
    mu = jnp.mean(v, axis=-1, keepdims=True)
    dv = v - mu
    var = jnp.mean(dv * dv, axis=-1, keepdims=True)
    vn = (dv * lax.rsqrt(var + eps) * g_ref[...] + b_ref[...]).astype(ws_ref.dtype)
    groups = ws_ref.shape[0]
    gw = vn.shape[1] // groups
    for g in range(groups):
        cols = slice(g * gw, (g + 1) * gw)
        vm = jnp.dot(ws_ref[g], vn[:, cols], preferred_element_type=F32) + bs_ref[:, g:g + 1]
        o_ref[:, cols] = (u_ref[:, cols].astype(F32) * vm).astype(o_ref.dtype)


def _chunk_mlp(h, w_in, ln_g, ln_b, w_s, b_s, w_out):
    t, d = h.shape
    uv = _mm(h, w_in, 2 * d, 512, mode="gelu")
    tc = CM_CHUNK
    groups = w_s.shape[0]
    vec_spec = pl.BlockSpec((1, d), lambda i: (0, 0))
    gated = _call(
        functools.partial(_cm_kernel, eps=LN_EPS), grid=(t // tc,),
        in_specs=[pl.BlockSpec((tc, d), lambda i: (i, 0)), pl.BlockSpec((tc, d), lambda i: (i, 1)),
                  vec_spec, vec_spec,
                  pl.BlockSpec((groups, tc, tc), lambda i: (0, 0, 0)),
                  pl.BlockSpec((tc, groups), lambda i: (0, 0))],
        out_specs=pl.BlockSpec((tc, d), lambda i: (i, 0)),
        out_shape=jax.ShapeDtypeStruct((t, d), BF16), name="chunk_gate",
    )(uv, uv, ln_g.reshape(1, d), ln_b.reshape(1, d), w_s.astype(BF16), b_s.T)
    return _mm(gated, w_out, d, 512)


def _na_bias_kernel(rpb_ref, o_ref, *, kh):
    cq = lax.broadcasted_iota(jnp.int32, (GRID_W, GRID_W), 0)
    ck = lax.broadcasted_iota(jnp.int32, (GRID_W, GRID_W), 1)
    cs = jnp.clip(cq - WIN_W // 2, 0, GRID_W - WIN_W)
    valid = jnp.logical_and(ck >= cs, ck < cs + WIN_W)
    tiles = []
    for dr in range(2 * WIN_H - 1):
        v = jnp.broadcast_to(rpb_ref[dr:dr + 1, :], (GRID_W, LANES)) * LOG2E
        t = pltpu.roll(v, LANES - (WIN_W - 1), axis=1, stride=1, stride_axis=0)[:, :GRID_W]
        tiles.append(jnp.where(valid, t, NEG_BIAS))
    for off in range(kh):
        o_ref[off] = jnp.concatenate([tiles[i - off + WIN_H - 1] for i in range(kh)], axis=1)


def _na_bias_table(rpb, kh):
    heads, nr, nc = rpb.shape
    assert nr <= 16 and nc <= LANES and GRID_W + WIN_W <= LANES
    padded = jnp.zeros((heads, 16, LANES), F32).at[:, :nr, :nc].set(rpb.astype(F32))
    return _call(
        functools.partial(_na_bias_kernel, kh=kh), grid=(heads,),
        in_specs=[pl.BlockSpec((None, 16, LANES), lambda hd: (hd, 0, 0))],
        out_specs=pl.BlockSpec((kh, None, GRID_W, kh * GRID_W), lambda hd: (0, hd, 0, 0)),
        out_shape=jax.ShapeDtypeStruct((kh, heads, GRID_W, kh * GRID_W), F32), name="nbr_bias",
    )(padded)


def _na_kernel(*refs, dh, kh, nrows, ctx, rq):
    bm_refs = refs[:rq]
    q_ref, k_ref, v_ref, o_ref = refs[rq:]
    nloc = kh * GRID_W
    for qi in range(rq):
        r = pl.program_id(2) * rq + qi
        rs = jnp.clip(r - kh // 2, 0, nrows - kh)
        start = pl.multiple_of(ctx + rs * GRID_W, GRID_W)
        qrows = slice(qi * GRID_W, (qi + 1) * GRID_W)
        for hd in range(q_ref.shape[1] // dh):
            cols = slice(hd * dh, (hd + 1) * dh)
            qh = q_ref[qrows, cols]
            kl = k_ref[pl.ds(start, nloc), cols]
            vl = v_ref[pl.ds(start, nloc), cols]
            kc = k_ref[0:ctx, cols]
            vc = v_ref[0:ctx, cols]
            sl = lax.dot_general(qh, kl, _NT, preferred_element_type=F32) + bm_refs[qi][hd]
            sc = lax.dot_general(qh, kc, _NT, preferred_element_type=F32)
            m = jnp.maximum(jnp.max(sl, axis=-1, keepdims=True), jnp.max(sc, axis=-1, keepdims=True))
            el = jnp.exp2(sl - m)
            ec = jnp.exp2(sc - m)
            den = jnp.sum(el, axis=-1, keepdims=True) + jnp.sum(ec, axis=-1, keepdims=True)
            o = (jnp.dot(el.astype(vl.dtype), vl, preferred_element_type=F32)
                 + jnp.dot(ec.astype(vc.dtype), vc, preferred_element_type=F32))
            o_ref[qrows, cols] = (o * (1.0 / den)).astype(o_ref.dtype)


def _neighbourhood_attention(h, rows, w_qkv, w_o, rpb):
    t, d = h.shape
    heads = rpb.shape[0]
    dh = d // heads
    nrows = rows.lat // GRID_W
    kh = min(WIN_H, nrows)
    qkv = _mm(h, w_qkv, 3 * d, 512, q_cols=d, q_scale=dh ** -0.5 * LOG2E)
    bm = _na_bias_table(rpb, kh)
    hg = 4
    rq = 2
    hw = hg * dh
    s, c = rows.seg, rows.ctx
    assert nrows % rq == 0

    def bm_spec(qi):
        def index(b, g, r):
            row = r * rq + qi
            return (row - jnp.clip(row - kh // 2, 0, nrows - kh), g, 0, 0)
        return pl.BlockSpec((None, hg, GRID_W, kh * GRID_W), index)

    o = _call(
        functools.partial(_na_kernel, dh=dh, kh=kh, nrows=nrows, ctx=c, rq=rq),
        grid=(rows.batch, heads // hg, nrows // rq),
        in_specs=[bm_spec(qi) for qi in range(rq)] + [
            pl.BlockSpec((rq * GRID_W, hw), lambda b, g, r: ((b * s + c) // (rq * GRID_W) + r, g)),
            pl.BlockSpec((s, hw), lambda b, g, r: (b, d // hw + g)),
            pl.BlockSpec((s, hw), lambda b, g, r: (b, 2 * d // hw + g))],
        out_specs=pl.BlockSpec((rq * GRID_W, hw), lambda b, g, r: (b * (nrows // rq) + r, g)),
        out_shape=jax.ShapeDtypeStruct((rows.batch * rows.lat, d), BF16), name="nbr_attn",
    )(*([bm] * rq), qkv, qkv, qkv)
    return _mm(o, w_o, d, 512)


def kernel(x, c, ctx, c_ctx, ada_down, ada_up, ada_b, ln_g, ln_b, ffn_w_in, ffn_w_out,
           ssm_a_re, ssm_a_im, ssm_log_dt, ssm_b_re, ssm_b_im, ssm_c_re, ssm_c_im, ssm_d,
           ssm_w_glu, da_w_qkv, da_w_o, da_lambda, da_subln_g,
           cm_w_in, cm_ln_g, cm_ln_b, cm_w_s, cm_b_s, cm_w_out,
           na_w_qkv, na_w_o, na_rpb):
    bsz, seq, d = x.shape
    nctx = ctx.shape[1]
    depth = ada_down.shape[0]
    assert depth == 4 and bsz + 1 <= 8, "one layer per mixer kind; the last layer drops the context stream"
    assert nctx % ROW_TILE == 0 and seq % ROW_TILE == 0 and seq % GRID_W == 0
    alpha = (2 * depth) ** 0.25

    cond = jnp.zeros((8, d), F32).at[:bsz].set(c).at[bsz].set(c_ctx)
    mods = _adaln(cond, ada_down, ada_up, ada_b).reshape(depth, 8, 1, 6 * d)
    uni = _Rows(bsz, nctx, seq, True)
    lat = _Rows(bsz, nctx, seq, False)

    xs = jnp.concatenate([ctx, x], axis=1).reshape(uni.total, d)
    h = _modulate(xs, mods[0], uni, 0, 1)
    for i in range(depth):
        rows = uni
        if i == 0:
            y = _s5_mixer(h, uni, ssm_a_re[0], ssm_a_im[0], ssm_log_dt[0], ssm_b_re[0], ssm_b_im[0],
                          ssm_c_re[0], ssm_c_im[0], ssm_d[0], ssm_w_glu[0])
        elif i == 1:
            y = _diff_attention(h, uni, da_w_qkv[0], da_w_o[0], da_lambda[0], da_subln_g[0],
                                0.8 - 0.6 * math.exp(-0.3 * i))
        elif i == 2:
            y = _chunk_mlp(h, cm_w_in[0], cm_ln_g[0], cm_ln_b[0], cm_w_s[0], cm_b_s[0], cm_w_out[0])
        else:
            y = _neighbourhood_attention(h, uni, na_w_qkv[0], na_w_o[0], na_rpb[0])
            rows = lat
        xs, h = _ln_res(xs, uni, y, rows, mods[i], 2, ln_g[i, 0], ln_b[i, 0], alpha,
                        next_mods=mods[i], next_chunks=(3, 4))
        y = _ffn(h, ffn_w_in[i], ffn_w_out[i])
        if i + 1 < depth:
            xs, h = _ln_res(xs, rows, y, rows, mods[i], 5, ln_g[i, 1], ln_b[i, 1], alpha,
                            next_mods=mods[i + 1], next_chunks=(0, 1))
        else:
            xs, _ = _ln_res(xs, rows, y, rows, mods[i], 5, ln_g[i, 1], ln_b[i, 1], alpha)
    return xs.reshape(bsz, seq, d)
```

```python
import functools
import math
from typing import NamedTuple

import jax
import jax.numpy as jnp
from jax import lax
from jax.experimental import pallas as pl
from jax.experimental.pallas import tpu as pltpu

F32 = jnp.float32
BF16 = jnp.bfloat16

GRID_W = 64
LN_EPS = 1e-5
SSM_CHUNK = 16
ROPE_THETA = 10000.0
CM_CHUNK = 128
WIN_H = 8
WIN_W = 16
NEG_BIAS = -1e30
LOG2E = 1.4426950408889634

LANES = 128
ROW_TILE = 256
EW_ROW_TILE = 128
VMEM_LIMIT_MB = 56

_NT = (((1,), (1,)), ((), ()))


class _Rows(NamedTuple):
    batch: int
    ctx: int
    lat: int
    with_ctx: bool

    @property
    def seg(self):
        return (self.ctx if self.with_ctx else 0) + self.lat

    @property
    def total(self):
        return self.batch * self.seg

    def mod_row(self, i, tm):
        per = self.seg // tm
        b = i // per
        if not self.with_ctx:
            return b
        return jnp.where(i % per < self.ctx // tm, self.batch, b)

    def tile_in(self, other, i, tm):
        if other.with_ctx == self.with_ctx:
            return i
        assert other.with_ctx and not self.with_ctx
        return i + (i // (self.lat // tm) + 1) * (self.ctx // tm)


def _call(kernel, *, grid, in_specs, out_specs, out_shape, scratch_shapes=(), aliases=None, name=None):
    return pl.pallas_call(
        kernel, grid=grid, in_specs=in_specs, out_specs=out_specs, out_shape=out_shape,
        scratch_shapes=list(scratch_shapes), input_output_aliases=aliases or {}, name=name,
        compiler_params=pltpu.CompilerParams(
            dimension_semantics=("arbitrary",) * len(grid),
            vmem_limit_bytes=VMEM_LIMIT_MB << 20))


def _gelu(x):
    return 0.5 * x * (1.0 + lax.erf(x * (2.0 ** -0.5)))


def _adaln_kernel(c_ref, wd_ref, wu_ref, b_ref, o_ref):
    c = c_ref[...]
    s = c * jax.nn.sigmoid(c)
    t = jnp.dot(s, wd_ref[...], precision=lax.Precision.HIGHEST, preferred_element_type=F32)
    o_ref[...] = jnp.dot(t, wu_ref[...], precision=lax.Precision.HIGHEST,
                         preferred_element_type=F32) + b_ref[...]


def _adaln(cond, w_down, w_up, b):
    depth, d, r = w_down.shape
    n = w_up.shape[2]
    tn = d
    return _call(
        _adaln_kernel, grid=(depth, n // tn),
        in_specs=[pl.BlockSpec((8, d), lambda l, j: (0, 0)),
                  pl.BlockSpec((None, d, r), lambda l, j: (l, 0, 0)),
                  pl.BlockSpec((None, r, tn), lambda l, j: (l, 0, j)),
                  pl.BlockSpec((None, 1, tn), lambda l, j: (l, 0, j))],
        out_specs=pl.BlockSpec((None, 8, tn), lambda l, j: (l, 0, j)),
        out_shape=jax.ShapeDtypeStruct((depth, 8, n), F32), name="adaln",
    )(cond, w_down, w_up, b.reshape(depth, 1, n))


def _mod_spec(rows, tm, d, chunk):
    return pl.BlockSpec((None, 1, d), lambda i: (rows.mod_row(i, tm), 0, chunk))


def _ln_res_kernel(*refs, alpha, eps, has_next):
    if has_next:
        x_ref, y_ref, gate_ref, g_ref, b_ref, sh_ref, sc_ref, xo_ref, ho_ref = refs
    else:
        x_ref, y_ref, gate_ref, g_ref, b_ref, xo_ref = refs
    z = alpha * x_ref[...] + gate_ref[...] * y_ref[...].astype(F32)
    mu = jnp.mean(z, axis=-1, keepdims=True)
    dz = z - mu
    var = jnp.mean(dz * dz, axis=-1, keepdims=True)
    xn = dz * lax.rsqrt(var + eps) * g_ref[...] + b_ref[...]
    xo_ref[...] = xn
    if has_next:
        ho_ref[...] = (xn * (1.0 + sc_ref[...]) + sh_ref[...]).astype(ho_ref.dtype)


def _ln_res(x, x_rows, y, rows, gate_mods, gate_chunk, ln_g, ln_b, alpha, next_mods=None, next_chunks=None):
    t, d = y.shape
    tm = EW_ROW_TILE
    has_next = next_mods is not None
    row_spec = pl.BlockSpec((tm, d), lambda i: (i, 0))
    vec_spec = pl.BlockSpec((1, d), lambda i: (0, 0))
    in_specs = [pl.BlockSpec((tm, d), lambda i: (rows.tile_in(x_rows, i, tm), 0)), row_spec,
                _mod_spec(rows, tm, d, gate_chunk), vec_spec, vec_spec]
    args = [x, y, gate_mods, ln_g.reshape(1, d), ln_b.reshape(1, d)]
    out_specs = [row_spec]
    out_shape = [jax.ShapeDtypeStruct((t, d), F32)]
    if has_next:
        in_specs += [_mod_spec(rows, tm, d, next_chunks[0]), _mod_spec(rows, tm, d, next_chunks[1])]
        args += [next_mods, next_mods]
        out_specs.append(row_spec)
        out_shape.append(jax.ShapeDtypeStruct((t, d), BF16))
    out = _call(
        functools.partial(_ln_res_kernel, alpha=alpha, eps=LN_EPS, has_next=has_next),
        grid=(t // tm,), in_specs=in_specs, out_specs=out_specs, out_shape=out_shape, name="ln_res",
    )(*args)
    return (out[0], out[1]) if has_next else (out[0], None)


def _swap32(a):
    lane = lax.broadcasted_iota(jnp.int32, a.shape, 1)
    return jnp.where(lane % 64 < 32, pltpu.roll(a, LANES - 32, axis=1), pltpu.roll(a, 32, axis=1))


def _mm_kernel(*refs, mode, nw, cast, q_tiles, q_scale):
    w_refs = refs[1:1 + nw]
    o_ref = refs[-1 - (nw if cast else 0)]
    if cast:
        scr = refs[len(refs) - nw:]

        @pl.when(pl.program_id(1) == 0)
        def _():
            for w_ref, s_ref in zip(w_refs, scr):
                s_ref[...] = w_ref[...].astype(s_ref.dtype)

        w_refs = scr
    x = refs[0][...]
    a = jnp.dot(x, w_refs[0][...], preferred_element_type=F32)
    if mode == "plain":
        out = a
    elif mode == "gelu":
        out = _gelu(a)
    elif mode == "swiglu":
        b = jnp.dot(x, w_refs[1][...], preferred_element_type=F32)
        out = a * jax.nn.sigmoid(a) * b
    elif mode == "glu":
        b = jnp.dot(x, w_refs[1][...], preferred_element_type=F32)
        out = a * jax.nn.sigmoid(b)
    elif mode == "rope":
        cos = refs[1 + nw][...]
        sin = refs[2 + nw][...]
        pieces = []
        for g in range(a.shape[1] // LANES):
            ag = a[:, g * LANES:(g + 1) * LANES]
            pieces.append(ag * cos + _swap32(ag) * sin)
        out = jnp.concatenate(pieces, axis=1)
    else:
        raise ValueError(mode)
    if q_tiles:
        out = out * jnp.where(pl.program_id(0) < q_tiles, q_scale, 1.0)
    o_ref[...] = out.astype(o_ref.dtype)


def _row_tile(t, k):
    budget = (24 << 20) if k <= 8192 else (12 << 20)
    for nt in range(1, t // 16 + 1):
        tm = t // nt
        if t % nt == 0 and tm % 16 == 0 and 4 * tm * k <= budget:
            return tm
    raise ValueError((t, k))


def _mm(x, w, n_out, tn, *, mode="plain", layer=None, col0=0, second_col=None, rope=None, q_cols=0,
        q_scale=1.0, out_dtype=BF16):
    t, k = x.shape
    tm = _row_tile(t, k)
    assert n_out % tn == 0 and col0 % tn == 0 and q_cols % tn == 0
    cast = w.dtype != BF16
    offs = [col0 // tn] + ([second_col // tn] if mode in ("swiglu", "glu") else [])
    in_specs = [pl.BlockSpec((tm, k), lambda j, i: (i, 0))]
    if layer is None:
        in_specs += [pl.BlockSpec((k, tn), lambda j, i, off=off: (0, j + off)) for off in offs]
    else:
        in_specs += [pl.BlockSpec((None, k, tn), lambda j, i, off=off: (layer, 0, j + off)) for off in offs]
    args = [x] + [w] * len(offs)
    if mode == "rope":
        in_specs += [pl.BlockSpec((tm, LANES), lambda j, i: (i, 0))] * 2
        args += list(rope)
    return _call(
        functools.partial(_mm_kernel, mode=mode, nw=len(offs), cast=cast, q_tiles=q_cols // tn, q_scale=q_scale),
        grid=(n_out // tn, t // tm),
        in_specs=in_specs, out_specs=pl.BlockSpec((tm, tn), lambda j, i: (i, j)),
        out_shape=jax.ShapeDtypeStruct((t, n_out), out_dtype),
        scratch_shapes=[pltpu.VMEM((k, tn), BF16)] * len(offs) if cast else (), name="mm_" + mode,
    )(*args)


def _ffn(h, w_in, w_out, layer):
    f = w_out.shape[1]
    a = _mm(h, w_in, f, 256, mode="swiglu", layer=layer, second_col=f)
    return _mm(a, w_out, w_out.shape[2], 512, layer=layer)


def _cmul(ar, ai, br, bi):
    return ar * br - ai * bi, ar * bi + ai * br


def _s5_param_kernel(are_ref, aim_ref, ldt_ref, btr_ref, bti_ref, cre_ref, cim_ref,
                     kc_ref, bz_ref, coef_ref, *, q):
    cre = cre_ref[...]
    cim = cim_ref[...]
    gc, p = cre.shape
    w = q * gc
    lane = lax.broadcasted_iota(jnp.int32, (gc, w), 1)
    kmat = None
    cmt = []
    bz = []
    coef_rows = []
    for d in range(2):
        lr = are_ref[d]
        li = aim_ref[d]
        dt = jnp.exp(ldt_ref[d])
        ea = jnp.exp(lr * dt)
        ar = ea * jnp.cos(li * dt)
        ai = ea * jnp.sin(li * dt)
        den = lr * lr + li * li
        nr = ar - 1.0
        kr = (nr * lr + ai * li) / den
        ki = (ai * lr - nr * li) / den
        bbr, bbi = _cmul(kr, ki, btr_ref[...], bti_ref[...])
        pr = [jnp.ones_like(ar)]
        pi = [jnp.zeros_like(ar)]
        for _ in range(q):
            nr_, ni_ = _cmul(pr[-1], pi[-1], ar, ai)
            pr.append(nr_)
            pi.append(ni_)
        cp = [_cmul(cre, cim, pr[k], pi[k]) for k in range(q + 1)]
        bp = [_cmul(bbr, bbi, pr[k], pi[k]) for k in range(q)]
        bb = jnp.concatenate([bbr, -bbi], axis=1)
        if d == 0:
            vall = jnp.concatenate([jnp.concatenate(cp[l], axis=1) for l in range(q)], axis=0)
        else:
            vall = jnp.concatenate([jnp.concatenate(cp[q - 1 - l], axis=1) for l in range(q)], axis=0)
        mt = lax.dot_general(bb, vall, _NT, precision=lax.Precision.HIGHEST,
                             preferred_element_type=F32)
        blocks = []
        for s in range(q):
            if d == 0:
                n = gc * s
                blk = mt if n == 0 else jnp.where(lane >= n, pltpu.roll(mt, n, axis=1), 0.0)
            else:
                n = gc * (q - 1 - s)
                blk = mt if n == 0 else jnp.where(lane < w - n, pltpu.roll(mt, w - n, axis=1), 0.0)
            blocks.append(blk)
        kd = jnp.concatenate(blocks, axis=0)
        kmat = kd if kmat is None else kmat + kd
        zs = [bp[q - 1 - s] if d == 0 else bp[s] for s in range(q)]
        bz.append(jnp.concatenate([jnp.concatenate([zr, zi], axis=1) for zr, zi in zs], axis=0))
        bz.append(jnp.concatenate([jnp.concatenate([zi, zr], axis=1) for zr, zi in zs], axis=0))
        cs = [cp[t + 1] if d == 0 else cp[q - t] for t in range(q)]
        cmt.append(jnp.concatenate([jnp.concatenate([vr, -vi], axis=1) for vr, vi in cs], axis=0))
        coef_rows.append(jnp.concatenate([pr[q], pr[q]], axis=1))
        coef_rows.append(jnp.concatenate([-pi[q], pi[q]], axis=1))
    kc_ref[0:w, :] = kmat.astype(kc_ref.dtype)
    cm = jnp.concatenate(cmt, axis=1).T
    kc_ref[w:w + 4 * p, :] = cm.astype(kc_ref.dtype)
    bz_ref[...] = jnp.concatenate(bz, axis=1).astype(bz_ref.dtype)
    coef_rows += [jnp.zeros_like(coef_rows[0])] * 4
    coef_ref[...] = jnp.concatenate(coef_rows, axis=0)


def _s5_params(a_re, a_im, log_dt, b_re, b_im, c_re, c_im):
    _, g, p = a_re.shape
    gc = c_re.shape[1]
    q = SSM_CHUNK
    w = q * gc
    assert 2 * p == LANES
    dir_spec = pl.BlockSpec((2, None, 1, p), lambda i: (0, i, 0, 0))
    mat_spec = pl.BlockSpec((None, gc, p), lambda i: (i, 0, 0))
    ldt = jnp.broadcast_to(log_dt[:, :, None, None], (2, g, 1, p))
    return _call(
        functools.partial(_s5_param_kernel, q=q), grid=(g,),
        in_specs=[dir_spec, dir_spec, dir_spec, mat_spec, mat_spec, mat_spec, mat_spec],
        out_specs=[pl.BlockSpec((None, w + 4 * p, w), lambda i: (i, 0, 0)),
                   pl.BlockSpec((None, w, 8 * p), lambda i: (i, 0, 0)),
                   pl.BlockSpec((None, 8, 2 * p), lambda i: (i, 0, 0))],
        out_shape=[jax.ShapeDtypeStruct((g, w + 4 * p, w), BF16),
                   jax.ShapeDtypeStruct((g, w, 8 * p), BF16),
                   jax.ShapeDtypeStruct((g, 8, 2 * p), F32)], name="s5_params",
    )(a_re.reshape(2, g, 1, p), a_im.reshape(2, g, 1, p), ldt,
      jnp.swapaxes(b_re, 1, 2), jnp.swapaxes(b_im, 1, 2), c_re, c_im)


def _s5_scan_kernel(u_ref, kc_ref, bz_ref, coef_ref, y_ref, z_scr, s_scr, *, nb, nsub, nsub_ctx):
    u = u_ref[...]
    w = u.shape[1]
    z_scr[...] = jnp.dot(u, bz_ref[...], preferred_element_type=F32)
    p1 = (coef_ref[0:1, :], coef_ref[2:3, :])
    p2 = (coef_ref[1:2, :], coef_ref[3:4, :])

    sub = 8
    ntile, ntile_ctx = nsub // sub, nsub_ctx // sub

    def body(n, carry):
        tb = jnp.where(n < ntile_ctx, ntile_ctx - 1 - n, ntile + ntile_ctx - 1 - n)
        new = []
        for b in range(nb):
            for d, tile in ((0, n), (1, tb)):
                h, hs = carry[2 * (2 * b + d)], carry[2 * (2 * b + d) + 1]
                base = pl.multiple_of(b * nsub + tile * sub, sub)
                lo = 2 * LANES * d
                zt = z_scr[pl.ds(base, sub), lo:lo + LANES]
                zst = z_scr[pl.ds(base, sub), lo + LANES:lo + 2 * LANES]
                states = [None] * sub
                for i in (range(sub) if d == 0 else reversed(range(sub))):
                    states[i] = h
                    h, hs = (p1[d] * h + p2[d] * hs + zt[i:i + 1], p1[d] * hs - p2[d] * h + zst[i:i + 1])
                s_scr[pl.ds(base, sub), LANES * d:LANES * (d + 1)] = jnp.concatenate(states, axis=0)
                new += [h, hs]
        return tuple(new)

    zero = jnp.zeros((1, LANES), F32)
    lax.fori_loop(0, ntile, body, (zero,) * (4 * nb))
    y = jnp.dot(u, kc_ref[0:w, :], preferred_element_type=F32)
    y += jnp.dot(s_scr[...].astype(u.dtype), kc_ref[w:w + 2 * LANES, :], preferred_element_type=F32)
    y_ref[...] = y.astype(y_ref.dtype)


def _group_lane_masks(shape, gc):
    lane_group = lax.broadcasted_iota(jnp.int32, shape, 1) // gc
    return [lane_group == g for g in range(LANES // gc)]


def _s5_pack_kernel(x_ref, sh_ref, sc_ref, h_ref, u_ref, h_scr, *, q, gc):
    h = x_ref[...] * (1.0 + sc_ref[...]) + sh_ref[...]
    h_ref[...] = h.astype(h_ref.dtype)
    tm, d = h.shape
    nr = tm // q
    gpb = LANES // gc
    masks = _group_lane_masks((nr, LANES), gc)
    for blk in range(d // LANES):
        h_scr[blk] = h[:, blk * LANES:(blk + 1) * LANES]
        acc = [[None] * (q // gpb) for _ in range(gpb)]
        for s in range(q):
            xs = h_scr[blk, pl.ds(s, nr, stride=q), :]
            sb, col = s % gpb, s // gpb
            for dl in range(gpb):
                g = (sb - dl) % gpb
                moved = xs if dl == 0 else pltpu.roll(xs, dl * gc, axis=1)
                prev = acc[g][col]
                acc[g][col] = jnp.where(masks[sb], moved, 0.0 if prev is None else prev)
        for g in range(gpb):
            u_ref[blk * gpb + g] = jnp.concatenate(acc[g], axis=1).astype(u_ref.dtype)


def _s5_finish_kernel(y_ref, h_ref, d_ref, o_ref, y_scr, *, q, gc):
    nr = y_ref.shape[1]
    d = h_ref.shape[1]
    gpb = LANES // gc
    masks = _group_lane_masks((nr, LANES), gc)
    for blk in range(d // LANES):
        lanes = slice(blk * LANES, (blk + 1) * LANES)
        ys = [y_ref[blk * gpb + g].astype(F32) for g in range(gpb)]
        for s in range(q):
            sb, col = s % gpb, s // gpb
            xs = None
            for dl in range(gpb):
                g = (sb + dl) % gpb
                piece = ys[g][:, col * LANES:(col + 1) * LANES]
                moved = piece if dl == 0 else pltpu.roll(piece, dl * gc, axis=1)
                xs = jnp.where(masks[g], moved, 0.0 if xs is None else xs)
            y_scr[blk, pl.ds(s, nr, stride=q), :] = xs
        v = (y_scr[blk] + d_ref[:, lanes] * h_ref[:, lanes].astype(F32)).astype(h_ref.dtype)
        o_ref[:, lanes] = _gelu(v.astype(F32)).astype(o_ref.dtype)


def _s5_mixer(x, mods, rows, a_re, a_im, log_dt, b_re, b_im, c_re, c_im, d_skip, w_glu):
    t, d = x.shape
    g = a_re.shape[1]
    gc = d // g
    q = SSM_CHUNK
    w = q * gc
    r = t // q
    assert LANES % gc == 0 and w % LANES == 0 and q % (LANES // gc) == 0
    kc, bz, coef = _s5_params(a_re, a_im, log_dt, b_re, b_im, c_re, c_im)
    tm = ROW_TILE
    nr = tm // q
    row_spec = pl.BlockSpec((tm, d), lambda i: (i, 0))
    grp_spec = pl.BlockSpec((g, nr, w), lambda i: (0, i, 0))
    h, u = _call(
        functools.partial(_s5_pack_kernel, q=q, gc=gc), grid=(t // tm,),
        in_specs=[row_spec, _mod_spec(rows, tm, d, 0), _mod_spec(rows, tm, d, 1)],
        out_specs=[row_spec, grp_spec],
        out_shape=[jax.ShapeDtypeStruct((t, d), BF16), jax.ShapeDtypeStruct((g, r, w), BF16)],
        scratch_shapes=[pltpu.VMEM((d // LANES, tm, LANES), F32)], name="s5_pack",
    )(x, mods, mods)
    y = _call(
        functools.partial(_s5_scan_kernel, nb=rows.batch, nsub=rows.seg // q, nsub_ctx=rows.ctx // q),
        grid=(g,),
        in_specs=[pl.BlockSpec((None, r, w), lambda i: (i, 0, 0)),
                  pl.BlockSpec((None,) + kc.shape[1:], lambda i: (i, 0, 0)),
                  pl.BlockSpec((None,) + bz.shape[1:], lambda i: (i, 0, 0)),
                  pl.BlockSpec((None,) + coef.shape[1:], lambda i: (i, 0, 0))],
        out_specs=pl.BlockSpec((None, r, w), lambda i: (i, 0, 0)),
        out_shape=jax.ShapeDtypeStruct((g, r, w), BF16),
        scratch_shapes=[pltpu.VMEM((r, 4 * LANES), F32), pltpu.VMEM((r, 2 * LANES), F32)],
        name="s5_scan",
    )(u, kc, bz, coef)
    v = _call(
        functools.partial(_s5_finish_kernel, q=q, gc=gc), grid=(t // tm,),
        in_specs=[grp_spec, row_spec, pl.BlockSpec((1, d), lambda i: (0, 0))],
        out_specs=row_spec, out_shape=jax.ShapeDtypeStruct((t, d), BF16),
        scratch_shapes=[pltpu.VMEM((d // LANES, tm, LANES), F32)], name="s5_finish",
    )(y, h, d_skip.reshape(1, d))
    return _mm(v, w_glu, d, 256, mode="glu", second_col=d)


def _rope_tables(rows):
    half = LANES // 2
    quarter = half // 2
    tpos = jnp.arange(rows.lat)
    pos = jnp.stack([tpos // GRID_W, tpos % GRID_W], axis=1).astype(F32)
    inv = ROPE_THETA ** (-jnp.arange(quarter, dtype=F32) * 2.0 / half)
    ang = pos[:, :, None] * inv[None, None, :]
    ang = jnp.repeat(ang[:, :, None, :], 2, axis=2).reshape(rows.lat, LANES)
    sign = jnp.tile(jnp.repeat(jnp.array([-1.0, 1.0], F32), quarter), 2)
    cos = jnp.cos(ang)
    sin = jnp.sin(ang) * sign
    cos = jnp.concatenate([jnp.ones((rows.ctx, LANES), F32), cos], axis=0)
    sin = jnp.concatenate([jnp.zeros((rows.ctx, LANES), F32), sin], axis=0)
    return jnp.tile(cos, (rows.batch, 1)), jnp.tile(sin, (rows.batch, 1))


def _da_kernel(lam_ref, g_ref, q_ref, k_ref, v_ref, o_ref, s_scr, e_scr, *, lam_init, eps, tk):
    lp = lam_ref[...]
    dh = lp.shape[1]
    lam = (jnp.exp(jnp.sum(lp[0:1] * lp[1:2], axis=-1, keepdims=True))
           - jnp.exp(jnp.sum(lp[2:3] * lp[3:4], axis=-1, keepdims=True)) + lam_init)
    tq = q_ref.shape[0]
    nchunk = k_ref.shape[0] // tk
    v = v_ref[...]
    outs = []
    for i in range(2):
        qi = q_ref[:, i * dh:(i + 1) * dh]
        m = jnp.full((tq, LANES), -jnp.inf, F32)
        for c in range(nchunk):
            s = lax.dot_general(qi, k_ref[c * tk:(c + 1) * tk, i * dh:(i + 1) * dh], _NT,
                                preferred_element_type=F32)
            s_scr[i, :, c * tk:(c + 1) * tk] = s
            for j in range(tk // LANES):
                m = jnp.maximum(m, s[:, j * LANES:(j + 1) * LANES])
        mrow = jnp.max(m, axis=-1, keepdims=True)
        den = jnp.zeros((tq, LANES), F32)
        for c in range(nchunk):
            e = jnp.exp2(s_scr[i, :, c * tk:(c + 1) * tk] - mrow)
            for j in range(tk // LANES):
                den = den + e[:, j * LANES:(j + 1) * LANES]
            e_scr[i, :, c * tk:(c + 1) * tk] = e.astype(e_scr.dtype)
        den = jnp.sum(den, axis=-1, keepdims=True)
        outs.append(jnp.dot(e_scr[i], v, preferred_element_type=F32) * (1.0 / den))
    o = outs[0] - lam * outs[1]
    on = o * lax.rsqrt(jnp.mean(o * o, axis=-1, keepdims=True) + eps)
    o_ref[...] = (on * g_ref[...] * (1.0 - lam_init)).astype(o_ref.dtype)


def _diff_attention(h, rows, w_qkv, w_o, lam_p, subln_g, lam_init):
    t, d = h.shape
    dh = lam_p.shape[1]
    hw = 2 * dh
    heads = d // hw
    cos, sin = _rope_tables(rows)
    qk = _mm(h, w_qkv, 2 * d, 512, mode="rope", rope=(cos, sin), q_cols=d, q_scale=dh ** -0.5 * LOG2E)
    v = _mm(h, w_qkv, d, 512, col0=2 * d)
    tk = ROW_TILE
    body = functools.partial(_da_kernel, lam_init=lam_init, eps=LN_EPS, tk=tk)

    def scratch(nq, nk):
        return [pltpu.VMEM((2, nq, nk), F32), pltpu.VMEM((2, nq, nk), BF16)]

    lam_spec = pl.BlockSpec(lam_p.shape, lambda b, hd, i: (0, 0))
    g_spec = pl.BlockSpec((1, hw), lambda b, hd, i: (0, 0))
    tq = ROW_TILE
    s, c = rows.seg, rows.ctx
    o = _call(
        body, grid=(rows.batch, heads, rows.lat // tq),
        in_specs=[lam_spec, g_spec,
                  pl.BlockSpec((tq, hw), lambda b, hd, i: ((b * s + c) // tq + i, hd)),
                  pl.BlockSpec((s, hw), lambda b, hd, i: (b, heads + hd)),
                  pl.BlockSpec((s, hw), lambda b, hd, i: (b, hd))],
        out_specs=pl.BlockSpec((tq, hw), lambda b, hd, i: ((b * s + c) // tq + i, hd)),
        out_shape=jax.ShapeDtypeStruct((t, d), BF16), scratch_shapes=scratch(tq, s), name="diff_attn_lat",
    )(lam_p, subln_g.reshape(1, hw), qk, qk, v)
    o = _call(
        lambda lam_ref, g_ref, q_ref, k_ref, v_ref, prev_ref, o_ref, s_scr, e_scr: body(
            lam_ref, g_ref, q_ref, k_ref, v_ref, o_ref, s_scr, e_scr),
        grid=(rows.batch, heads, 1),
        in_specs=[lam_spec, g_spec,
                  pl.BlockSpec((c, hw), lambda b, hd, i: (b * s // c, hd)),
                  pl.BlockSpec((c, hw), lambda b, hd, i: (b * s // c, heads + hd)),
                  pl.BlockSpec((c, hw), lambda b, hd, i: (b * s // c, hd)),
                  pl.BlockSpec(memory_space=pl.ANY)],
        out_specs=pl.BlockSpec((c, hw), lambda b, hd, i: (b * s // c, hd)),
        out_shape=jax.ShapeDtypeStruct((t, d), BF16), aliases={5: 0}, scratch_shapes=scratch(c, c),
        name="diff_attn_ctx",
    )(lam_p, subln_g.reshape(1, hw), qk, qk, v, o)
    return _mm(o, w_o, d, 512)


def _cm_kernel(u_ref, v_ref, g_ref, b_ref, ws_ref, bs_ref, o_ref, *, eps):
    v = v_ref[...].astype(F32)
    mu = jnp.mean(v, axis=-1, keepdims=True)
    dv = v - mu
    var = jnp.mean(dv * dv, axis=-1, keepdims=True)
    vn = (dv * lax.rsqrt(var + eps) * g_ref[...] + b_ref[...]).astype(ws_ref.dtype)
    groups = ws_ref.shape[0]
    gw = vn.shape[1] // groups
    for g in range(groups):
        cols = slice(g * gw, (g + 1) * gw)
        vm = jnp.dot(ws_ref[g], vn[:, cols], preferred_element_type=F32) + bs_ref[:, g:g + 1]
        o_ref[:, cols] = (u_ref[:, cols].astype(F32) * vm).astype(o_ref.dtype)


def _chunk_mlp(h, w_in, ln_g, ln_b, w_s, b_s, w_out):
    t, d = h.shape
    uv = _mm(h, w_in, 2 * d, 512, mode="gelu")
    tc = CM_CHUNK
    groups = w_s.shape[0]
    vec_spec = pl.BlockSpec((1, d), lambda i: (0, 0))
    gated = _call(
        functools.partial(_cm_kernel, eps=LN_EPS), grid=(t // tc,),
        in_specs=[pl.BlockSpec((tc, d), lambda i: (i, 0)), pl.BlockSpec((tc, d), lambda i: (i, 1)),
                  vec_spec, vec_spec,
                  pl.BlockSpec((groups, tc, tc), lambda i: (0, 0, 0)),
                  pl.BlockSpec((tc, groups), lambda i: (0, 0))],
        out_specs=pl.BlockSpec((tc, d), lambda i: (i, 0)),
        out_shape=jax.ShapeDtypeStruct((t, d), BF16), name="chunk_gate",
    )(uv, uv, ln_g.reshape(1, d), ln_b.reshape(1, d), w_s.astype(BF16), b_s.T)
    return _mm(gated, w_out, d, 512)


def _na_bias_kernel(rpb_ref, o_ref, *, kh):
    cq = lax.broadcasted_iota(jnp.int32, (GRID_W, GRID_W), 0)
    ck = lax.broadcasted_iota(jnp.int32, (GRID_W, GRID_W), 1)
    cs = jnp.clip(cq - WIN_W // 2, 0, GRID_W - WIN_W)
    valid = jnp.logical_and(ck >= cs, ck < cs + WIN_W)
    tiles = []
    for dr in range(2 * WIN_H - 1):
        v = jnp.broadcast_to(rpb_ref[dr:dr + 1, :], (GRID_W, LANES)) * LOG2E
        t = pltpu.roll(v, LANES - (WIN_W - 1), axis=1, stride=1, stride_axis=0)[:, :GRID_W]
        tiles.append(jnp.where(valid, t, NEG_BIAS))
    for off in range(kh):
        o_ref[off] = jnp.concatenate([tiles[i - off + WIN_H - 1] for i in range(kh)], axis=1)


def _na_bias_table(rpb, kh):
    heads, nr, nc = rpb.shape
    assert nr <= 16 and nc <= LANES and GRID_W + WIN_W <= LANES
    padded = jnp.zeros((heads, 16, LANES), F32).at[:, :nr, :nc].set(rpb.astype(F32))
    return _call(
        functools.partial(_na_bias_kernel, kh=kh), grid=(heads,),
        in_specs=[pl.BlockSpec((None, 16, LANES), lambda hd: (hd, 0, 0))],
        out_specs=pl.BlockSpec((kh, None, GRID_W, kh * GRID_W), lambda hd: (0, hd, 0, 0)),
        out_shape=jax.ShapeDtypeStruct((kh, heads, GRID_W, kh * GRID_W), F32), name="nbr_bias",
    )(padded)


def _na_kernel(*refs, dh, kh, nrows, ctx, rq):
    bm_refs = refs[:rq]
    q_ref, k_ref, v_ref, o_ref = refs[rq:]
    nloc = kh * GRID_W
    for qi in range(rq):
        r = pl.program_id(2) * rq + qi
        rs = jnp.clip(r - kh // 2, 0, nrows - kh)
        start = pl.multiple_of(ctx + rs * GRID_W, GRID_W)
        qrows = slice(qi * GRID_W, (qi + 1) * GRID_W)
        for hd in range(q_ref.shape[1] // dh):
            cols = slice(hd * dh, (hd + 1) * dh)
            qh = q_ref[qrows, cols]
            kl = k_ref[pl.ds(start, nloc), cols]
            vl = v_ref[pl.ds(start, nloc), cols]
            kc = k_ref[0:ctx, cols]
            vc = v_ref[0:ctx, cols]
            sl = lax.dot_general(qh, kl, _NT, preferred_element_type=F32) + bm_refs[qi][hd]
            sc = lax.dot_general(qh, kc, _NT, preferred_element_type=F32)
            m = jnp.maximum(jnp.max(sl, axis=-1, keepdims=True), jnp.max(sc, axis=-1, keepdims=True))
            el = jnp.exp2(sl - m)
            ec = jnp.exp2(sc - m)
            den = jnp.sum(el, axis=-1, keepdims=True) + jnp.sum(ec, axis=-1, keepdims=True)
            o = (jnp.dot(el.astype(vl.dtype), vl, preferred_element_type=F32)
                 + jnp.dot(ec.astype(vc.dtype), vc, preferred_element_type=F32))
            o_ref[qrows, cols] = (o * (1.0 / den)).astype(o_ref.dtype)


def _neighbourhood_attention(h, rows, w_qkv, w_o, rpb):
    t, d = h.shape
    heads = rpb.shape[0]
    dh = d // heads
    nrows = rows.lat // GRID_W
    kh = min(WIN_H, nrows)
    qkv = _mm(h, w_qkv, 3 * d, 512, q_cols=d, q_scale=dh ** -0.5 * LOG2E)
    bm = _na_bias_table(rpb, kh)
    hg = 4
    rq = 2
    hw = hg * dh
    s, c = rows.seg, rows.ctx
    assert nrows % rq == 0

    def bm_spec(qi):
        def index(b, g, r):
            row = r * rq + qi
            return (row - jnp.clip(row - kh // 2, 0, nrows - kh), g, 0, 0)
        return pl.BlockSpec((None, hg, GRID_W, kh * GRID_W), index)

    o = _call(
        functools.partial(_na_kernel, dh=dh, kh=kh, nrows=nrows, ctx=c, rq=rq),
        grid=(rows.batch, heads // hg, nrows // rq),
        in_specs=[bm_spec(qi) for qi in range(rq)] + [
            pl.BlockSpec((rq * GRID_W, hw), lambda b, g, r: ((b * s + c) // (rq * GRID_W) + r, g)),
            pl.BlockSpec((s, hw), lambda b, g, r: (b, d // hw + g)),
            pl.BlockSpec((s, hw), lambda b, g, r: (b, 2 * d // hw + g))],
        out_specs=pl.BlockSpec((rq * GRID_W, hw), lambda b, g, r: (b * (nrows // rq) + r, g)),
        out_shape=jax.ShapeDtypeStruct((rows.batch * rows.lat, d), BF16), name="nbr_attn",
    )(*([bm] * rq), qkv, qkv, qkv)
    return _mm(o, w_o, d, 512)


def kernel(x, c, ctx, c_ctx, ada_down, ada_up, ada_b, ln_g, ln_b, ffn_w_in, ffn_w_out,
           ssm_a_re, ssm_a_im, ssm_log_dt, ssm_b_re, ssm_b_im, ssm_c_re, ssm_c_im, ssm_d,
           ssm_w_glu, da_w_qkv, da_w_o, da_lambda, da_subln_g,
           cm_w_in, cm_ln_g, cm_ln_b, cm_w_s, cm_b_s, cm_w_out,
           na_w_qkv, na_w_o, na_rpb):
    bsz, seq, d = x.shape
    nctx = ctx.shape[1]
    depth = ada_down.shape[0]
    assert depth == 4 and bsz + 1 <= 8, "one layer per mixer kind; the last layer drops the context stream"
    assert nctx % ROW_TILE == 0 and seq % ROW_TILE == 0 and seq % GRID_W == 0
    alpha = (2 * depth) ** 0.25

    cond = jnp.zeros((8, d), F32).at[:bsz].set(c).at[bsz].set(c_ctx)
    mods = _adaln(cond, ada_down, ada_up, ada_b).reshape(depth, 8, 1, 6 * d)
    uni = _Rows(bsz, nctx, seq, True)
    lat = _Rows(bsz, nctx, seq, False)

    xs = jnp.concatenate([ctx, x], axis=1).reshape(uni.total, d)
    w_out_bf16 = ffn_w_out.astype(BF16)
    h = None
    for i in range(depth):
        rows = uni
        if i == 0:
            y = _s5_mixer(xs, mods[0], uni, ssm_a_re[0], ssm_a_im[0], ssm_log_dt[0], ssm_b_re[0], ssm_b_im[0],
                          ssm_c_re[0], ssm_c_im[0], ssm_d[0], ssm_w_glu[0])
        elif i == 1:
            y = _diff_attention(h, uni, da_w_qkv[0], da_w_o[0], da_lambda[0], da_subln_g[0],
                                0.8 - 0.6 * math.exp(-0.3 * i))
        elif i == 2:
            y = _chunk_mlp(h, cm_w_in[0], cm_ln_g[0], cm_ln_b[0], cm_w_s[0], cm_b_s[0], cm_w_out[0])
        else:
            y = _neighbourhood_attention(h, uni, na_w_qkv[0], na_w_o[0], na_rpb[0])
            rows = lat
        xs, h = _ln_res(xs, uni, y, rows, mods[i], 2, ln_g[i, 0], ln_b[i, 0], alpha,
                        next_mods=mods[i], next_chunks=(3, 4))
        y = _ffn(h, ffn_w_in, w_out_bf16, i)
        if i + 1 < depth:
            xs, h = _ln_res(xs, rows, y, rows, mods[i], 5, ln_g[i, 1], ln_b[i, 1], alpha,
                            next_mods=mods[i + 1], next_chunks=(0, 1))
        else:
            xs, _ = _ln_res(xs, rows, y, rows, mods[i], 5, ln_g[i, 1], ln_b[i, 1], alpha)
    return xs.reshape(bsz, seq, d)
```

```python
import functools
import math
from typing import NamedTuple

import jax
import jax.numpy as jnp
from jax import lax
from jax.experimental import pallas as pl
from jax.experimental.pallas import tpu as pltpu

F32 = jnp.float32
BF16 = jnp.bfloat16

GRID_W = 64
LN_EPS = 1e-5
SSM_CHUNK = 16
ROPE_THETA = 10000.0
CM_CHUNK = 128
WIN_H = 8
WIN_W = 16
NEG_BIAS = -1e30
LOG2E = 1.4426950408889634

LANES = 128
ROW_TILE = 256
EW_ROW_TILE = 128
VMEM_LIMIT_MB = 56

_NT = (((1,), (1,)), ((), ()))


class _Rows(NamedTuple):
    batch: int
    ctx: int
    lat: int
    with_ctx: bool

    @property
    def seg(self):
        return (self.ctx if self.with_ctx else 0) + self.lat

    @property
    def total(self):
        return self.batch * self.seg

    def mod_row(self, i, tm):
        per = self.seg // tm
        b = i // per
        if not self.with_ctx:
            return b
        return jnp.where(i % per < self.ctx // tm, self.batch, b)

    def tile_in(self, other, i, tm):
        if other.with_ctx == self.with_ctx:
            return i
        assert other.with_ctx and not self.with_ctx
        return i + (i // (self.lat // tm) + 1) * (self.ctx // tm)


def _call(kernel, *, grid, in_specs, out_specs, out_shape, scratch_shapes=(), aliases=None, name=None):
    return pl.pallas_call(
        kernel, grid=grid, in_specs=in_specs, out_specs=out_specs, out_shape=out_shape,
        scratch_shapes=list(scratch_shapes), input_output_aliases=aliases or {}, name=name,
        compiler_params=pltpu.CompilerParams(
            dimension_semantics=("arbitrary",) * len(grid),
            vmem_limit_bytes=VMEM_LIMIT_MB << 20))


def _gelu(x):
    return 0.5 * x * (1.0 + lax.erf(x * (2.0 ** -0.5)))


def _adaln_kernel(c_ref, wd_ref, wu_ref, b_ref, o_ref):
    c = c_ref[...]
    s = c * jax.nn.sigmoid(c)
    t = jnp.dot(s, wd_ref[...], precision=lax.Precision.HIGHEST, preferred_element_type=F32)
    o_ref[...] = jnp.dot(t, wu_ref[...], precision=lax.Precision.HIGHEST,
                         preferred_element_type=F32) + b_ref[...]


def _adaln(cond, w_down, w_up, b):
    depth, d, r = w_down.shape
    n = w_up.shape[2]
    tn = d
    return _call(
        _adaln_kernel, grid=(depth, n // tn),
        in_specs=[pl.BlockSpec((8, d), lambda l, j: (0, 0)),
                  pl.BlockSpec((None, d, r), lambda l, j: (l, 0, 0)),
                  pl.BlockSpec((None, r, tn), lambda l, j: (l, 0, j)),
                  pl.BlockSpec((None, 1, tn), lambda l, j: (l, 0, j))],
        out_specs=pl.BlockSpec((None, 8, tn), lambda l, j: (l, 0, j)),
        out_shape=jax.ShapeDtypeStruct((depth, 8, n), F32), name="adaln",
    )(cond, w_down, w_up, b.reshape(depth, 1, n))


def _mod_spec(rows, tm, d, chunk):
    return pl.BlockSpec((None, 1, d), lambda i: (rows.mod_row(i, tm), 0, chunk))


def _ln_res_kernel(*refs, alpha, eps, has_next):
    if has_next:
        x_ref, y_ref, gate_ref, g_ref, b_ref, sh_ref, sc_ref, xo_ref, ho_ref = refs
    else:
        x_ref, y_ref, gate_ref, g_ref, b_ref, xo_ref = refs
    z = alpha * x_ref[...] + gate_ref[...] * y_ref[...].astype(F32)
    mu = jnp.mean(z, axis=-1, keepdims=True)
    dz = z - mu
    var = jnp.mean(dz * dz, axis=-1, keepdims=True)
    xn = dz * lax.rsqrt(var + eps) * g_ref[...] + b_ref[...]
    xo_ref[...] = xn
    if has_next:
        ho_ref[...] = (xn * (1.0 + sc_ref[...]) + sh_ref[...]).astype(ho_ref.dtype)


def _ln_res(x, x_rows, y, rows, gate_mods, gate_chunk, ln_g, ln_b, alpha, next_mods=None, next_chunks=None):
    t, d = y.shape
    tm = EW_ROW_TILE
    has_next = next_mods is not None
    row_spec = pl.BlockSpec((tm, d), lambda i: (i, 0))
    vec_spec = pl.BlockSpec((1, d), lambda i: (0, 0))
    in_specs = [pl.BlockSpec((tm, d), lambda i: (rows.tile_in(x_rows, i, tm), 0)), row_spec,
                _mod_spec(rows, tm, d, gate_chunk), vec_spec, vec_spec]
    args = [x, y, gate_mods, ln_g.reshape(1, d), ln_b.reshape(1, d)]
    out_specs = [row_spec]
    out_shape = [jax.ShapeDtypeStruct((t, d), F32)]
    if has_next:
        in_specs += [_mod_spec(rows, tm, d, next_chunks[0]), _mod_spec(rows, tm, d, next_chunks[1])]
        args += [next_mods, next_mods]
        out_specs.append(row_spec)
        out_shape.append(jax.ShapeDtypeStruct((t, d), BF16))
    out = _call(
        functools.partial(_ln_res_kernel, alpha=alpha, eps=LN_EPS, has_next=has_next),
        grid=(t // tm,), in_specs=in_specs, out_specs=out_specs, out_shape=out_shape, name="ln_res",
    )(*args)
    return (out[0], out[1]) if has_next else (out[0], None)


def _swap32(a):
    lane = lax.broadcasted_iota(jnp.int32, a.shape, 1)
    return jnp.where(lane % 64 < 32, pltpu.roll(a, LANES - 32, axis=1), pltpu.roll(a, 32, axis=1))


def _mm_kernel(*refs, mode, nw, cast, q_tiles, q_scale):
    w_refs = refs[1:1 + nw]
    o_ref = refs[-1 - (nw if cast else 0)]
    if cast:
        scr = refs[len(refs) - nw:]

        @pl.when(pl.program_id(1) == 0)
        def _():
            for w_ref, s_ref in zip(w_refs, scr):
                s_ref[...] = w_ref[...].astype(s_ref.dtype)

        w_refs = scr
    x = refs[0][...]
    a = jnp.dot(x, w_refs[0][...], preferred_element_type=F32)
    if mode == "plain":
        out = a
    elif mode == "gelu":
        out = _gelu(a)
    elif mode == "swiglu":
        b = jnp.dot(x, w_refs[1][...], preferred_element_type=F32)
        out = a * jax.nn.sigmoid(a) * b
    elif mode == "glu":
        b = jnp.dot(x, w_refs[1][...], preferred_element_type=F32)
        out = a * jax.nn.sigmoid(b)
    elif mode == "rope":
        cos = refs[1 + nw][...]
        sin = refs[2 + nw][...]
        pieces = []
        for g in range(a.shape[1] // LANES):
            ag = a[:, g * LANES:(g + 1) * LANES]
            pieces.append(ag * cos + _swap32(ag) * sin)
        out = jnp.concatenate(pieces, axis=1)
    else:
        raise ValueError(mode)
    if q_tiles:
        out = out * jnp.where(pl.program_id(0) < q_tiles, q_scale, 1.0)
    o_ref[...] = out.astype(o_ref.dtype)


def _row_tile(t, k):
    budget = (24 << 20) if k <= 8192 else (12 << 20)
    for nt in range(1, t // 16 + 1):
        tm = t // nt
        if t % nt == 0 and tm % 16 == 0 and 4 * tm * k <= budget:
            return tm
    raise ValueError((t, k))


def _mm(x, w, n_out, tn, *, mode="plain", layer=None, col0=0, second_col=None, rope=None, q_cols=0,
        q_scale=1.0, out_dtype=BF16):
    t, k = x.shape
    tm = _row_tile(t, k)
    assert n_out % tn == 0 and col0 % tn == 0 and q_cols % tn == 0
    cast = w.dtype != BF16
    offs = [col0 // tn] + ([second_col // tn] if mode in ("swiglu", "glu") else [])
    in_specs = [pl.BlockSpec((tm, k), lambda j, i: (i, 0))]
    if layer is None:
        in_specs += [pl.BlockSpec((k, tn), lambda j, i, off=off: (0, j + off)) for off in offs]
    else:
        in_specs += [pl.BlockSpec((None, k, tn), lambda j, i, off=off: (layer, 0, j + off)) for off in offs]
    args = [x] + [w] * len(offs)
    if mode == "rope":
        in_specs += [pl.BlockSpec((tm, LANES), lambda j, i: (i, 0))] * 2
        args += list(rope)
    return _call(
        functools.partial(_mm_kernel, mode=mode, nw=len(offs), cast=cast, q_tiles=q_cols // tn, q_scale=q_scale),
        grid=(n_out // tn, t // tm),
        in_specs=in_specs, out_specs=pl.BlockSpec((tm, tn), lambda j, i: (i, j)),
        out_shape=jax.ShapeDtypeStruct((t, n_out), out_dtype),
        scratch_shapes=[pltpu.VMEM((k, tn), BF16)] * len(offs) if cast else (), name="mm_" + mode,
    )(*args)


def _ffn(h, w_in, w_out, layer):
    f = w_out.shape[1]
    a = _mm(h, w_in, f, 256, mode="swiglu", layer=layer, second_col=f)
    return _mm(a, w_out, w_out.shape[2], 512, layer=layer)


def _cmul(ar, ai, br, bi):
    return ar * br - ai * bi, ar * bi + ai * br


def _s5_param_kernel(are_ref, aim_ref, ldt_ref, btr_ref, bti_ref, cre_ref, cim_ref,
                     kc_ref, bz_ref, coef_ref, *, q):
    cre = cre_ref[...]
    cim = cim_ref[...]
    gc, p = cre.shape
    w = q * gc
    lane = lax.broadcasted_iota(jnp.int32, (gc, w), 1)
    kmat = None
    cmt = []
    bz = []
    coef_rows = []
    for d in range(2):
        lr = are_ref[d]
        li = aim_ref[d]
        dt = jnp.exp(ldt_ref[d])
        ea = jnp.exp(lr * dt)
        ar = ea * jnp.cos(li * dt)
        ai = ea * jnp.sin(li * dt)
        den = lr * lr + li * li
        nr = ar - 1.0
        kr = (nr * lr + ai * li) / den
        ki = (ai * lr - nr * li) / den
        bbr, bbi = _cmul(kr, ki, btr_ref[...], bti_ref[...])
        pr = [jnp.ones_like(ar)]
        pi = [jnp.zeros_like(ar)]
        for _ in range(q):
            nr_, ni_ = _cmul(pr[-1], pi[-1], ar, ai)
            pr.append(nr_)
            pi.append(ni_)
        cp = [_cmul(cre, cim, pr[k], pi[k]) for k in range(q + 1)]
        bp = [_cmul(bbr, bbi, pr[k], pi[k]) for k in range(q)]
        bb = jnp.concatenate([bbr, -bbi], axis=1)
        if d == 0:
            vall = jnp.concatenate([jnp.concatenate(cp[l], axis=1) for l in range(q)], axis=0)
        else:
            vall = jnp.concatenate([jnp.concatenate(cp[q - 1 - l], axis=1) for l in range(q)], axis=0)
        mt = lax.dot_general(bb, vall, _NT, precision=lax.Precision.HIGHEST,
                             preferred_element_type=F32)
        blocks = []
        for s in range(q):
            if d == 0:
                n = gc * s
                blk = mt if n == 0 else jnp.where(lane >= n, pltpu.roll(mt, n, axis=1), 0.0)
            else:
                n = gc * (q - 1 - s)
                blk = mt if n == 0 else jnp.where(lane < w - n, pltpu.roll(mt, w - n, axis=1), 0.0)
            blocks.append(blk)
        kd = jnp.concatenate(blocks, axis=0)
        kmat = kd if kmat is None else kmat + kd
        zs = [bp[q - 1 - s] if d == 0 else bp[s] for s in range(q)]
        bz.append(jnp.concatenate([jnp.concatenate([zr, zi], axis=1) for zr, zi in zs], axis=0))
        bz.append(jnp.concatenate([jnp.concatenate([zi, zr], axis=1) for zr, zi in zs], axis=0))
        cs = [cp[t + 1] if d == 0 else cp[q - t] for t in range(q)]
        cmt.append(jnp.concatenate([jnp.concatenate([vr, -vi], axis=1) for vr, vi in cs], axis=0))
        coef_rows.append(jnp.concatenate([pr[q], pr[q]], axis=1))
        coef_rows.append(jnp.concatenate([-pi[q], pi[q]], axis=1))
    kc_ref[0:w, :] = kmat.astype(kc_ref.dtype)
    cm = jnp.concatenate(cmt, axis=1).T
    kc_ref[w:w + 4 * p, :] = cm.astype(kc_ref.dtype)
    bz_ref[...] = jnp.concatenate(bz, axis=1).astype(bz_ref.dtype)
    coef_rows += [jnp.zeros_like(coef_rows[0])] * 4
    coef_ref[...] = jnp.concatenate(coef_rows, axis=0)


def _s5_params(a_re, a_im, log_dt, b_re, b_im, c_re, c_im):
    _, g, p = a_re.shape
    gc = c_re.shape[1]
    q = SSM_CHUNK
    w = q * gc
    assert 2 * p == LANES
    dir_spec = pl.BlockSpec((2, None, 1, p), lambda i: (0, i, 0, 0))
    mat_spec = pl.BlockSpec((None, gc, p), lambda i: (i, 0, 0))
    ldt = jnp.broadcast_to(log_dt[:, :, None, None], (2, g, 1, p))
    return _call(
        functools.partial(_s5_param_kernel, q=q), grid=(g,),
        in_specs=[dir_spec, dir_spec, dir_spec, mat_spec, mat_spec, mat_spec, mat_spec],
        out_specs=[pl.BlockSpec((None, w + 4 * p, w), lambda i: (i, 0, 0)),
                   pl.BlockSpec((None, w, 8 * p), lambda i: (i, 0, 0)),
                   pl.BlockSpec((None, 8, 2 * p), lambda i: (i, 0, 0))],
        out_shape=[jax.ShapeDtypeStruct((g, w + 4 * p, w), BF16),
                   jax.ShapeDtypeStruct((g, w, 8 * p), BF16),
                   jax.ShapeDtypeStruct((g, 8, 2 * p), F32)], name="s5_params",
    )(a_re.reshape(2, g, 1, p), a_im.reshape(2, g, 1, p), ldt,
      jnp.swapaxes(b_re, 1, 2), jnp.swapaxes(b_im, 1, 2), c_re, c_im)


def _s5_scan_kernel(u_ref, kc_ref, bz_ref, coef_ref, y_ref, z_scr, s_scr, *, nb, nsub, nsub_ctx):
    ng, _, w = u_ref.shape
    for g in range(ng):
        z_scr[g] = jnp.dot(u_ref[g], bz_ref[g], preferred_element_type=F32)
    p1 = [(coef_ref[g, 0:1, :], coef_ref[g, 2:3, :]) for g in range(ng)]
    p2 = [(coef_ref[g, 1:2, :], coef_ref[g, 3:4, :]) for g in range(ng)]

    sub = 8
    ntile, ntile_ctx = nsub // sub, nsub_ctx // sub

    def body(n, carry):
        tb = jnp.where(n < ntile_ctx, ntile_ctx - 1 - n, ntile + ntile_ctx - 1 - n)
        new = []
        k = 0
        for g in range(ng):
            for b in range(nb):
                for d, tile in ((0, n), (1, tb)):
                    h, hs = carry[k], carry[k + 1]
                    k += 2
                    base = pl.multiple_of(b * nsub + tile * sub, sub)
                    lo = 2 * LANES * d
                    zt = z_scr[g, pl.ds(base, sub), lo:lo + LANES]
                    zst = z_scr[g, pl.ds(base, sub), lo + LANES:lo + 2 * LANES]
                    states = [None] * sub
                    for i in (range(sub) if d == 0 else reversed(range(sub))):
                        states[i] = h
                        h, hs = (p1[g][d] * h + p2[g][d] * hs + zt[i:i + 1],
                                 p1[g][d] * hs - p2[g][d] * h + zst[i:i + 1])
                    s_scr[g, pl.ds(base, sub), LANES * d:LANES * (d + 1)] = jnp.concatenate(states, axis=0)
                    new += [h, hs]
        return tuple(new)

    zero = jnp.zeros((1, LANES), F32)
    lax.fori_loop(0, ntile, body, (zero,) * (4 * nb * ng))
    for g in range(ng):
        u = u_ref[g]
        y = jnp.dot(u, kc_ref[g, 0:w, :], preferred_element_type=F32)
        y += jnp.dot(s_scr[g].astype(u.dtype), kc_ref[g, w:w + 2 * LANES, :], preferred_element_type=F32)
        y_ref[g] = y.astype(y_ref.dtype)


def _group_lane_masks(shape, gc):
    lane_group = lax.broadcasted_iota(jnp.int32, shape, 1) // gc
    return [lane_group == g for g in range(LANES // gc)]


def _s5_pack_kernel(x_ref, sh_ref, sc_ref, h_ref, u_ref, h_scr, *, q, gc):
    h = x_ref[...] * (1.0 + sc_ref[...]) + sh_ref[...]
    h_ref[...] = h.astype(h_ref.dtype)
    tm, d = h.shape
    nr = tm // q
    gpb = LANES // gc
    masks = _group_lane_masks((nr, LANES), gc)
    for blk in range(d // LANES):
        h_scr[blk] = h[:, blk * LANES:(blk + 1) * LANES]
        acc = [[None] * (q // gpb) for _ in range(gpb)]
        for s in range(q):
            xs = h_scr[blk, pl.ds(s, nr, stride=q), :]
            sb, col = s % gpb, s // gpb
            for dl in range(gpb):
                g = (sb - dl) % gpb
                moved = xs if dl == 0 else pltpu.roll(xs, dl * gc, axis=1)
                prev = acc[g][col]
                acc[g][col] = jnp.where(masks[sb], moved, 0.0 if prev is None else prev)
        for g in range(gpb):
            u_ref[blk * gpb + g] = jnp.concatenate(acc[g], axis=1).astype(u_ref.dtype)


def _s5_finish_kernel(y_ref, h_ref, d_ref, o_ref, y_scr, *, q, gc):
    nr = y_ref.shape[1]
    d = h_ref.shape[1]
    gpb = LANES // gc
    masks = _group_lane_masks((nr, LANES), gc)
    for blk in range(d // LANES):
        lanes = slice(blk * LANES, (blk + 1) * LANES)
        ys = [y_ref[blk * gpb + g].astype(F32) for g in range(gpb)]
        for s in range(q):
            sb, col = s % gpb, s // gpb
            xs = None
            for dl in range(gpb):
                g = (sb + dl) % gpb
                piece = ys[g][:, col * LANES:(col + 1) * LANES]
                moved = piece if dl == 0 else pltpu.roll(piece, dl * gc, axis=1)
                xs = jnp.where(masks[g], moved, 0.0 if xs is None else xs)
            y_scr[blk, pl.ds(s, nr, stride=q), :] = xs
        v = (y_scr[blk] + d_ref[:, lanes] * h_ref[:, lanes].astype(F32)).astype(h_ref.dtype)
        o_ref[:, lanes] = _gelu(v.astype(F32)).astype(o_ref.dtype)


def _s5_mixer(x, mods, rows, a_re, a_im, log_dt, b_re, b_im, c_re, c_im, d_skip, w_glu):
    t, d = x.shape
    g = a_re.shape[1]
    gc = d // g
    q = SSM_CHUNK
    w = q * gc
    r = t // q
    assert LANES % gc == 0 and w % LANES == 0 and q % (LANES // gc) == 0
    kc, bz, coef = _s5_params(a_re, a_im, log_dt, b_re, b_im, c_re, c_im)
    tm = ROW_TILE
    nr = tm // q
    gps = 2
    assert g % gps == 0
    row_spec = pl.BlockSpec((tm, d), lambda i: (i, 0))
    grp_spec = pl.BlockSpec((g, nr, w), lambda i: (0, i, 0))
    h, u = _call(
        functools.partial(_s5_pack_kernel, q=q, gc=gc), grid=(t // tm,),
        in_specs=[row_spec, _mod_spec(rows, tm, d, 0), _mod_spec(rows, tm, d, 1)],
        out_specs=[row_spec, grp_spec],
        out_shape=[jax.ShapeDtypeStruct((t, d), BF16), jax.ShapeDtypeStruct((g, r, w), BF16)],
        scratch_shapes=[pltpu.VMEM((d // LANES, tm, LANES), F32)], name="s5_pack",
    )(x, mods, mods)
    y = _call(
        functools.partial(_s5_scan_kernel, nb=rows.batch, nsub=rows.seg // q, nsub_ctx=rows.ctx // q),
        grid=(g // gps,),
        in_specs=[pl.BlockSpec((gps, r, w), lambda i: (i, 0, 0)),
                  pl.BlockSpec((gps,) + kc.shape[1:], lambda i: (i, 0, 0)),
                  pl.BlockSpec((gps,) + bz.shape[1:], lambda i: (i, 0, 0)),
                  pl.BlockSpec((gps,) + coef.shape[1:], lambda i: (i, 0, 0))],
        out_specs=pl.BlockSpec((gps, r, w), lambda i: (i, 0, 0)),
        out_shape=jax.ShapeDtypeStruct((g, r, w), BF16),
        scratch_shapes=[pltpu.VMEM((gps, r, 4 * LANES), F32), pltpu.VMEM((gps, r, 2 * LANES), F32)],
        name="s5_scan",
    )(u, kc, bz, coef)
    v = _call(
        functools.partial(_s5_finish_kernel, q=q, gc=gc), grid=(t // tm,),
        in_specs=[grp_spec, row_spec, pl.BlockSpec((1, d), lambda i: (0, 0))],
        out_specs=row_spec, out_shape=jax.ShapeDtypeStruct((t, d), BF16),
        scratch_shapes=[pltpu.VMEM((d // LANES, tm, LANES), F32)], name="s5_finish",
    )(y, h, d_skip.reshape(1, d))
    return _mm(v, w_glu, d, 256, mode="glu", second_col=d)


def _rope_tables(rows):
    half = LANES // 2
    quarter = half // 2
    tpos = jnp.arange(rows.lat)
    pos = jnp.stack([tpos // GRID_W, tpos % GRID_W], axis=1).astype(F32)
    inv = ROPE_THETA ** (-jnp.arange(quarter, dtype=F32) * 2.0 / half)
    ang = pos[:, :, None] * inv[None, None, :]
    ang = jnp.repeat(ang[:, :, None, :], 2, axis=2).reshape(rows.lat, LANES)
    sign = jnp.tile(jnp.repeat(jnp.array([-1.0, 1.0], F32), quarter), 2)
    cos = jnp.cos(ang)
    sin = jnp.sin(ang) * sign
    cos = jnp.concatenate([jnp.ones((rows.ctx, LANES), F32), cos], axis=0)
    sin = jnp.concatenate([jnp.zeros((rows.ctx, LANES), F32), sin], axis=0)
    return jnp.tile(cos, (rows.batch, 1)), jnp.tile(sin, (rows.batch, 1))


def _da_kernel(lam_ref, g_ref, q_ref, k_ref, v_ref, o_ref, s_scr, e_scr, *, lam_init, eps, tk):
    lp = lam_ref[...]
    dh = lp.shape[1]
    lam = (jnp.exp(jnp.sum(lp[0:1] * lp[1:2], axis=-1, keepdims=True))
           - jnp.exp(jnp.sum(lp[2:3] * lp[3:4], axis=-1, keepdims=True)) + lam_init)
    tq = q_ref.shape[0]
    nchunk = k_ref.shape[0] // tk
    v = v_ref[...]
    row_max = []
    for i in range(2):
        qi = q_ref[:, i * dh:(i + 1) * dh]
        m = jnp.full((tq, LANES), -jnp.inf, F32)
        for c in range(nchunk):
            s = lax.dot_general(qi, k_ref[c * tk:(c + 1) * tk, i * dh:(i + 1) * dh], _NT,
                                preferred_element_type=F32)
            s_scr[i, :, c * tk:(c + 1) * tk] = s
            for j in range(tk // LANES):
                m = jnp.maximum(m, s[:, j * LANES:(j + 1) * LANES])
        row_max.append(jnp.max(m, axis=-1, keepdims=True))
    outs = []
    for i in range(2):
        mrow = row_max[i]
        den = jnp.zeros((tq, LANES), F32)
        for c in range(nchunk):
            e = jnp.exp2(s_scr[i, :, c * tk:(c + 1) * tk] - mrow)
            for j in range(tk // LANES):
                den = den + e[:, j * LANES:(j + 1) * LANES]
            e_scr[i, :, c * tk:(c + 1) * tk] = e.astype(e_scr.dtype)
        den = jnp.sum(den, axis=-1, keepdims=True)
        outs.append(jnp.dot(e_scr[i], v, preferred_element_type=F32) * (1.0 / den))
    o = outs[0] - lam * outs[1]
    on = o * lax.rsqrt(jnp.mean(o * o, axis=-1, keepdims=True) + eps)
    o_ref[...] = (on * g_ref[...] * (1.0 - lam_init)).astype(o_ref.dtype)


def _diff_attention(h, rows, w_qkv, w_o, lam_p, subln_g, lam_init):
    t, d = h.shape
    dh = lam_p.shape[1]
    hw = 2 * dh
    heads = d // hw
    cos, sin = _rope_tables(rows)
    qk = _mm(h, w_qkv, 2 * d, 512, mode="rope", rope=(cos, sin), q_cols=d, q_scale=dh ** -0.5 * LOG2E)
    v = _mm(h, w_qkv, d, 512, col0=2 * d)
    tk = ROW_TILE
    body = functools.partial(_da_kernel, lam_init=lam_init, eps=LN_EPS, tk=tk)

    def scratch(nq, nk):
        return [pltpu.VMEM((2, nq, nk), F32), pltpu.VMEM((2, nq, nk), BF16)]

    lam_spec = pl.BlockSpec(lam_p.shape, lambda b, hd, i: (0, 0))
    g_spec = pl.BlockSpec((1, hw), lambda b, hd, i: (0, 0))
    tq = ROW_TILE
    s, c = rows.seg, rows.ctx
    o = _call(
        body, grid=(rows.batch, heads, rows.lat // tq),
        in_specs=[lam_spec, g_spec,
                  pl.BlockSpec((tq, hw), lambda b, hd, i: ((b * s + c) // tq + i, hd)),
                  pl.BlockSpec((s, hw), lambda b, hd, i: (b, heads + hd)),
                  pl.BlockSpec((s, hw), lambda b, hd, i: (b, hd))],
        out_specs=pl.BlockSpec((tq, hw), lambda b, hd, i: ((b * s + c) // tq + i, hd)),
        out_shape=jax.ShapeDtypeStruct((t, d), BF16), scratch_shapes=scratch(tq, s), name="diff_attn_lat",
    )(lam_p, subln_g.reshape(1, hw), qk, qk, v)
    o = _call(
        lambda lam_ref, g_ref, q_ref, k_ref, v_ref, prev_ref, o_ref, s_scr, e_scr: body(
            lam_ref, g_ref, q_ref, k_ref, v_ref, o_ref, s_scr, e_scr),
        grid=(rows.batch, heads, 1),
        in_specs=[lam_spec, g_spec,
                  pl.BlockSpec((c, hw), lambda b, hd, i: (b * s // c, hd)),
                  pl.BlockSpec((c, hw), lambda b, hd, i: (b * s // c, heads + hd)),
                  pl.BlockSpec((c, hw), lambda b, hd, i: (b * s // c, hd)),
                  pl.BlockSpec(memory_space=pl.ANY)],
        out_specs=pl.BlockSpec((c, hw), lambda b, hd, i: (b * s // c, hd)),
        out_shape=jax.ShapeDtypeStruct((t, d), BF16), aliases={5: 0}, scratch_shapes=scratch(c, c),
        name="diff_attn_ctx",
    )(lam_p, subln_g.reshape(1, hw), qk, qk, v, o)
    return _mm(o, w_o, d, 512)


def _cm_kernel(u_ref, v_ref, g_ref, b_ref, ws_ref, bs_ref, o_ref, *, eps):
    v = v_ref[...].astype(F32)
    mu = jnp.mean(v, axis=-1, keepdims=True)
    dv = v - mu
    var = jnp.mean(dv * dv, axis=-1, keepdims=True)
    vn = (dv * lax.rsqrt(var + eps) * g_ref[...] + b_ref[...]).astype(ws_ref.dtype)
    groups = ws_ref.shape[0]
    gw = vn.shape[1] // groups
    for g in range(groups):
        cols = slice(g * gw, (g + 1) * gw)
        vm = jnp.dot(ws_ref[g], vn[:, cols], preferred_element_type=F32) + bs_ref[:, g:g + 1]
        o_ref[:, cols] = (u_ref[:, cols].astype(F32) * vm).astype(o_ref.dtype)


def _chunk_mlp(h, w_in, ln_g, ln_b, w_s, b_s, w_out):
    t, d = h.shape
    uv = _mm(h, w_in, 2 * d, 512, mode="gelu")
    tc = CM_CHUNK
    groups = w_s.shape[0]
    vec_spec = pl.BlockSpec((1, d), lambda i: (0, 0))
    gated = _call(
        functools.partial(_cm_kernel, eps=LN_EPS), grid=(t // tc,),
        in_specs=[pl.BlockSpec((tc, d), lambda i: (i, 0)), pl.BlockSpec((tc, d), lambda i: (i, 1)),
                  vec_spec, vec_spec,
                  pl.BlockSpec((groups, tc, tc), lambda i: (0, 0, 0)),
                  pl.BlockSpec((tc, groups), lambda i: (0, 0))],
        out_specs=pl.BlockSpec((tc, d), lambda i: (i, 0)),
        out_shape=jax.ShapeDtypeStruct((t, d), BF16), name="chunk_gate",
    )(uv, uv, ln_g.reshape(1, d), ln_b.reshape(1, d), w_s.astype(BF16), b_s.T)
    return _mm(gated, w_out, d, 512)


def _na_bias_kernel(rpb_ref, o_ref, *, kh):
    cq = lax.broadcasted_iota(jnp.int32, (GRID_W, GRID_W), 0)
    ck = lax.broadcasted_iota(jnp.int32, (GRID_W, GRID_W), 1)
    cs = jnp.clip(cq - WIN_W // 2, 0, GRID_W - WIN_W)
    valid = jnp.logical_and(ck >= cs, ck < cs + WIN_W)
    tiles = []
    for dr in range(2 * WIN_H - 1):
        v = jnp.broadcast_to(rpb_ref[dr:dr + 1, :], (GRID_W, LANES)) * LOG2E
        t = pltpu.roll(v, LANES - (WIN_W - 1), axis=1, stride=1, stride_axis=0)[:, :GRID_W]
        tiles.append(jnp.where(valid, t, NEG_BIAS))
    for off in range(kh):
        o_ref[off] = jnp.concatenate([tiles[i - off + WIN_H - 1] for i in range(kh)], axis=1)


def _na_bias_table(rpb, kh):
    heads, nr, nc = rpb.shape
    assert nr <= 16 and nc <= LANES and GRID_W + WIN_W <= LANES
    padded = jnp.zeros((heads, 16, LANES), F32).at[:, :nr, :nc].set(rpb.astype(F32))
    return _call(
        functools.partial(_na_bias_kernel, kh=kh), grid=(heads,),
        in_specs=[pl.BlockSpec((None, 16, LANES), lambda hd: (hd, 0, 0))],
        out_specs=pl.BlockSpec((kh, None, GRID_W, kh * GRID_W), lambda hd: (0, hd, 0, 0)),
        out_shape=jax.ShapeDtypeStruct((kh, heads, GRID_W, kh * GRID_W), F32), name="nbr_bias",
    )(padded)


def _na_kernel(*refs, dh, kh, nrows, ctx, rq):
    bm_refs = refs[:rq]
    q_ref, k_ref, v_ref, o_ref = refs[rq:]
    nloc = kh * GRID_W
    chains = []
    for qi in range(rq):
        r = pl.program_id(2) * rq + qi
        rs = jnp.clip(r - kh // 2, 0, nrows - kh)
        start = pl.multiple_of(ctx + rs * GRID_W, GRID_W)
        qrows = slice(qi * GRID_W, (qi + 1) * GRID_W)
        for hd in range(q_ref.shape[1] // dh):
            cols = slice(hd * dh, (hd + 1) * dh)
            qh = q_ref[qrows, cols]
            sl = lax.dot_general(qh, k_ref[pl.ds(start, nloc), cols], _NT,
                                 preferred_element_type=F32) + bm_refs[qi][hd]
            sc = lax.dot_general(qh, k_ref[0:ctx, cols], _NT, preferred_element_type=F32)
            chains.append((qrows, cols, start, sl, sc))
    probs = []
    for _, _, _, sl, sc in chains:
        m = jnp.maximum(jnp.max(sl, axis=-1, keepdims=True), jnp.max(sc, axis=-1, keepdims=True))
        el = jnp.exp2(sl - m)
        ec = jnp.exp2(sc - m)
        den = jnp.sum(el, axis=-1, keepdims=True) + jnp.sum(ec, axis=-1, keepdims=True)
        probs.append((el.astype(v_ref.dtype), ec.astype(v_ref.dtype), 1.0 / den))
    for (qrows, cols, start, _, _), (el, ec, inv) in zip(chains, probs):
        o = (jnp.dot(el, v_ref[pl.ds(start, nloc), cols], preferred_element_type=F32)
             + jnp.dot(ec, v_ref[0:ctx, cols], preferred_element_type=F32))
        o_ref[qrows, cols] = (o * inv).astype(o_ref.dtype)


def _neighbourhood_attention(h, rows, w_qkv, w_o, rpb):
    t, d = h.shape
    heads = rpb.shape[0]
    dh = d // heads
    nrows = rows.lat // GRID_W
    kh = min(WIN_H, nrows)
    qkv = _mm(h, w_qkv, 3 * d, 512, q_cols=d, q_scale=dh ** -0.5 * LOG2E)
    bm = _na_bias_table(rpb, kh)
    hg = 4
    rq = 2
    hw = hg * dh
    s, c = rows.seg, rows.ctx
    assert nrows % rq == 0

    def bm_spec(qi):
        def index(b, g, r):
            row = r * rq + qi
            return (row - jnp.clip(row - kh // 2, 0, nrows - kh), g, 0, 0)
        return pl.BlockSpec((None, hg, GRID_W, kh * GRID_W), index)

    o = _call(
        functools.partial(_na_kernel, dh=dh, kh=kh, nrows=nrows, ctx=c, rq=rq),
        grid=(rows.batch, heads // hg, nrows // rq),
        in_specs=[bm_spec(qi) for qi in range(rq)] + [
            pl.BlockSpec((rq * GRID_W, hw), lambda b, g, r: ((b * s + c) // (rq * GRID_W) + r, g)),
            pl.BlockSpec((s, hw), lambda b, g, r: (b, d // hw + g)),
            pl.BlockSpec((s, hw), lambda b, g, r: (b, 2 * d // hw + g))],
        out_specs=pl.BlockSpec((rq * GRID_W, hw), lambda b, g, r: (b * (nrows // rq) + r, g)),
        out_shape=jax.ShapeDtypeStruct((rows.batch * rows.lat, d), BF16), name="nbr_attn",
    )(*([bm] * rq), qkv, qkv, qkv)
    return _mm(o, w_o, d, 512)


def kernel(x, c, ctx, c_ctx, ada_down, ada_up, ada_b, ln_g, ln_b, ffn_w_in, ffn_w_out,
           ssm_a_re, ssm_a_im, ssm_log_dt, ssm_b_re, ssm_b_im, ssm_c_re, ssm_c_im, ssm_d,
           ssm_w_glu, da_w_qkv, da_w_o, da_lambda, da_subln_g,
           cm_w_in, cm_ln_g, cm_ln_b, cm_w_s, cm_b_s, cm_w_out,
           na_w_qkv, na_w_o, na_rpb):
    bsz, seq, d = x.shape
    nctx = ctx.shape[1]
    depth = ada_down.shape[0]
    assert depth == 4 and bsz + 1 <= 8, "one layer per mixer kind; the last layer drops the context stream"
    assert nctx % ROW_TILE == 0 and seq % ROW_TILE == 0 and seq % GRID_W == 0
    alpha = (2 * depth) ** 0.25

    cond = jnp.zeros((8, d), F32).at[:bsz].set(c).at[bsz].set(c_ctx)
    mods = _adaln(cond, ada_down, ada_up, ada_b).reshape(depth, 8, 1, 6 * d)
    uni = _Rows(bsz, nctx, seq, True)
    lat = _Rows(bsz, nctx, seq, False)

    xs = jnp.concatenate([ctx, x], axis=1).reshape(uni.total, d)
    w_out_bf16 = ffn_w_out.astype(BF16)
    h = None
    for i in range(depth):
        rows = uni
        if i == 0:
            y = _s5_mixer(xs, mods[0], uni, ssm_a_re[0], ssm_a_im[0], ssm_log_dt[0], ssm_b_re[0], ssm_b_im[0],
                          ssm_c_re[0], ssm_c_im[0], ssm_d[0], ssm_w_glu[0])
        elif i == 1:
            y = _diff_attention(h, uni, da_w_qkv[0], da_w_o[0], da_lambda[0], da_subln_g[0],
                                0.8 - 0.6 * math.exp(-0.3 * i))
        elif i == 2:
            y = _chunk_mlp(h, cm_w_in[0], cm_ln_g[0], cm_ln_b[0], cm_w_s[0], cm_b_s[0], cm_w_out[0])
        else:
            y = _neighbourhood_attention(h, uni, na_w_qkv[0], na_w_o[0], na_rpb[0])
            rows = lat
        xs, h = _ln_res(xs, uni, y, rows, mods[i], 2, ln_g[i, 0], ln_b[i, 0], alpha,
                        next_mods=mods[i], next_chunks=(3, 4))
        y = _ffn(h, ffn_w_in, w_out_bf16, i)
        if i + 1 < depth:
            xs, h = _ln_res(xs, rows, y, rows, mods[i], 5, ln_g[i, 1], ln_b[i, 1], alpha,
                            next_mods=mods[i + 1], next_chunks=(0, 1))
        else:
            xs, _ = _ln_res(xs, rows, y, rows, mods[i], 5, ln_g[i, 1], ln_b[i, 1], alpha)
    return xs.reshape(bsz, seq, d)
```

```python
import functools
import math
from typing import NamedTuple

import jax
import jax.numpy as jnp
from jax import lax
from jax.experimental import pallas as pl
from jax.experimental.pallas import tpu as pltpu

F32 = jnp.float32
BF16 = jnp.bfloat16

GRID_W = 64
LN_EPS = 1e-5
SSM_CHUNK = 16
ROPE_THETA = 10000.0
CM_CHUNK = 128
WIN_H = 8
WIN_W = 16
NEG_BIAS = -1e30
LOG2E = 1.4426950408889634

LANES = 128
ROW_TILE = 256
EW_ROW_TILE = 256
VMEM_LIMIT_MB = 56

_NT = (((1,), (1,)), ((), ()))


class _Rows(NamedTuple):
    batch: int
    ctx: int
    lat: int
    with_ctx: bool

    @property
    def seg(self):
        return (self.ctx if self.with_ctx else 0) + self.lat

    @property
    def total(self):
        return self.batch * self.seg

    def mod_row(self, i, tm):
        per = self.seg // tm
        b = i // per
        if not self.with_ctx:
            return b
        return jnp.where(i % per < self.ctx // tm, self.batch, b)

    def tile_in(self, other, i, tm):
        if other.with_ctx == self.with_ctx:
            return i
        assert other.with_ctx and not self.with_ctx
        return i + (i // (self.lat // tm) + 1) * (self.ctx // tm)


def _call(kernel, *, grid, in_specs, out_specs, out_shape, scratch_shapes=(), aliases=None, name=None):
    return pl.pallas_call(
        kernel, grid=grid, in_specs=in_specs, out_specs=out_specs, out_shape=out_shape,
        scratch_shapes=list(scratch_shapes), input_output_aliases=aliases or {}, name=name,
        compiler_params=pltpu.CompilerParams(
            dimension_semantics=("arbitrary",) * len(grid),
            vmem_limit_bytes=VMEM_LIMIT_MB << 20))


def _gelu(x):
    return 0.5 * x * (1.0 + lax.erf(x * (2.0 ** -0.5)))


def _adaln_kernel(c_ref, wd_ref, wu_ref, b_ref, o_ref):
    c = c_ref[...]
    s = c * jax.nn.sigmoid(c)
    t = jnp.dot(s, wd_ref[...], precision=lax.Precision.HIGHEST, preferred_element_type=F32)
    o_ref[...] = jnp.dot(t, wu_ref[...], precision=lax.Precision.HIGHEST,
                         preferred_element_type=F32) + b_ref[...]


def _adaln(cond, w_down, w_up, b):
    depth, d, r = w_down.shape
    n = w_up.shape[2]
    tn = d
    return _call(
        _adaln_kernel, grid=(depth, n // tn),
        in_specs=[pl.BlockSpec((8, d), lambda l, j: (0, 0)),
                  pl.BlockSpec((None, d, r), lambda l, j: (l, 0, 0)),
                  pl.BlockSpec((None, r, tn), lambda l, j: (l, 0, j)),
                  pl.BlockSpec((None, 1, tn), lambda l, j: (l, 0, j))],
        out_specs=pl.BlockSpec((None, 8, tn), lambda l, j: (l, 0, j)),
        out_shape=jax.ShapeDtypeStruct((depth, 8, n), F32), name="adaln",
    )(cond, w_down, w_up, b.reshape(depth, 1, n))


def _mod_spec(rows, tm, d, chunk):
    return pl.BlockSpec((None, 1, d), lambda i: (rows.mod_row(i, tm), 0, chunk))


def _ln_res_kernel(*refs, alpha, eps, has_next):
    if has_next:
        x_ref, y_ref, gate_ref, g_ref, b_ref, sh_ref, sc_ref, xo_ref, ho_ref = refs
    else:
        x_ref, y_ref, gate_ref, g_ref, b_ref, xo_ref = refs
    z = alpha * x_ref[...] + gate_ref[...] * y_ref[...].astype(F32)
    mu = jnp.mean(z, axis=-1, keepdims=True)
    dz = z - mu
    var = jnp.mean(dz * dz, axis=-1, keepdims=True)
    xn = dz * lax.rsqrt(var + eps) * g_ref[...] + b_ref[...]
    xo_ref[...] = xn
    if has_next:
        ho_ref[...] = (xn * (1.0 + sc_ref[...]) + sh_ref[...]).astype(ho_ref.dtype)


def _ln_res(x, x_rows, y, rows, gate_mods, gate_chunk, ln_g, ln_b, alpha, next_mods=None, next_chunks=None):
    t, d = y.shape
    tm = EW_ROW_TILE
    has_next = next_mods is not None
    row_spec = pl.BlockSpec((tm, d), lambda i: (i, 0))
    vec_spec = pl.BlockSpec((1, d), lambda i: (0, 0))
    in_specs = [pl.BlockSpec((tm, d), lambda i: (rows.tile_in(x_rows, i, tm), 0)), row_spec,
                _mod_spec(rows, tm, d, gate_chunk), vec_spec, vec_spec]
    args = [x, y, gate_mods, ln_g.reshape(1, d), ln_b.reshape(1, d)]
    out_specs = [row_spec]
    out_shape = [jax.ShapeDtypeStruct((t, d), F32)]
    if has_next:
        in_specs += [_mod_spec(rows, tm, d, next_chunks[0]), _mod_spec(rows, tm, d, next_chunks[1])]
        args += [next_mods, next_mods]
        out_specs.append(row_spec)
        out_shape.append(jax.ShapeDtypeStruct((t, d), BF16))
    out = _call(
        functools.partial(_ln_res_kernel, alpha=alpha, eps=LN_EPS, has_next=has_next),
        grid=(t // tm,), in_specs=in_specs, out_specs=out_specs, out_shape=out_shape, name="ln_res",
    )(*args)
    return (out[0], out[1]) if has_next else (out[0], None)


def _swap32(a):
    lane = lax.broadcasted_iota(jnp.int32, a.shape, 1)
    return jnp.where(lane % 64 < 32, pltpu.roll(a, LANES - 32, axis=1), pltpu.roll(a, 32, axis=1))


def _mm_kernel(*refs, mode, nw, cast, q_tiles, q_scale):
    w_refs = refs[1:1 + nw]
    o_ref = refs[-1 - (nw if cast else 0)]
    if cast:
        scr = refs[len(refs) - nw:]

        @pl.when(pl.program_id(1) == 0)
        def _():
            for w_ref, s_ref in zip(w_refs, scr):
                s_ref[...] = w_ref[...].astype(s_ref.dtype)

        w_refs = scr
    x = refs[0][...]
    a = jnp.dot(x, w_refs[0][...], preferred_element_type=F32)
    if mode == "plain":
        out = a
    elif mode == "gelu":
        out = _gelu(a)
    elif mode == "swiglu":
        b = jnp.dot(x, w_refs[1][...], preferred_element_type=F32)
        out = a * jax.nn.sigmoid(a) * b
    elif mode == "glu":
        b = jnp.dot(x, w_refs[1][...], preferred_element_type=F32)
        out = a * jax.nn.sigmoid(b)
    elif mode == "rope":
        cos = refs[1 + nw][...]
        sin = refs[2 + nw][...]
        pieces = []
        for g in range(a.shape[1] // LANES):
            ag = a[:, g * LANES:(g + 1) * LANES]
            pieces.append(ag * cos + _swap32(ag) * sin)
        out = jnp.concatenate(pieces, axis=1)
    else:
        raise ValueError(mode)
    if q_tiles:
        out = out * jnp.where(pl.program_id(0) < q_tiles, q_scale, 1.0)
    o_ref[...] = out.astype(o_ref.dtype)


def _row_tile(t, k):
    budget = (24 << 20) if k <= 8192 else (12 << 20)
    for nt in range(1, t // 16 + 1):
        tm = t // nt
        if t % nt == 0 and tm % 16 == 0 and 4 * tm * k <= budget:
            return tm
    raise ValueError((t, k))


def _mm(x, w, n_out, tn, *, mode="plain", layer=None, col0=0, second_col=None, rope=None, q_cols=0,
        q_scale=1.0, out_dtype=BF16):
    t, k = x.shape
    tm = _row_tile(t, k)
    assert n_out % tn == 0 and col0 % tn == 0 and q_cols % tn == 0
    cast = w.dtype != BF16
    offs = [col0 // tn] + ([second_col // tn] if mode in ("swiglu", "glu") else [])
    in_specs = [pl.BlockSpec((tm, k), lambda j, i: (i, 0))]
    if layer is None:
        in_specs += [pl.BlockSpec((k, tn), lambda j, i, off=off: (0, j + off)) for off in offs]
    else:
        in_specs += [pl.BlockSpec((None, k, tn), lambda j, i, off=off: (layer, 0, j + off)) for off in offs]
    args = [x] + [w] * len(offs)
    if mode == "rope":
        in_specs += [pl.BlockSpec((tm, LANES), lambda j, i: (i, 0))] * 2
        args += list(rope)
    return _call(
        functools.partial(_mm_kernel, mode=mode, nw=len(offs), cast=cast, q_tiles=q_cols // tn, q_scale=q_scale),
        grid=(n_out // tn, t // tm),
        in_specs=in_specs, out_specs=pl.BlockSpec((tm, tn), lambda j, i: (i, j)),
        out_shape=jax.ShapeDtypeStruct((t, n_out), out_dtype),
        scratch_shapes=[pltpu.VMEM((k, tn), BF16)] * len(offs) if cast else (), name="mm_" + mode,
    )(*args)


def _ffn(h, w_in, w_out, layer):
    f = w_out.shape[1]
    a = _mm(h, w_in, f, 256, mode="swiglu", layer=layer, second_col=f)
    return _mm(a, w_out, w_out.shape[2], 512, layer=layer)


def _cmul(ar, ai, br, bi):
    return ar * br - ai * bi, ar * bi + ai * br


def _s5_param_kernel(are_ref, aim_ref, ldt_ref, btr_ref, bti_ref, cre_ref, cim_ref,
                     kc_ref, bz_ref, coef_ref, *, q):
    cre = cre_ref[...]
    cim = cim_ref[...]
    gc, p = cre.shape
    w = q * gc
    lane = lax.broadcasted_iota(jnp.int32, (gc, w), 1)
    kmat = None
    cmt = []
    bz = []
    coef_rows = []
    for d in range(2):
        lr = are_ref[d]
        li = aim_ref[d]
        dt = jnp.exp(ldt_ref[d])
        ea = jnp.exp(lr * dt)
        ar = ea * jnp.cos(li * dt)
        ai = ea * jnp.sin(li * dt)
        den = lr * lr + li * li
        nr = ar - 1.0
        kr = (nr * lr + ai * li) / den
        ki = (ai * lr - nr * li) / den
        bbr, bbi = _cmul(kr, ki, btr_ref[...], bti_ref[...])
        pr = [jnp.ones_like(ar)]
        pi = [jnp.zeros_like(ar)]
        for _ in range(q):
            nr_, ni_ = _cmul(pr[-1], pi[-1], ar, ai)
            pr.append(nr_)
            pi.append(ni_)
        cp = [_cmul(cre, cim, pr[k], pi[k]) for k in range(q + 1)]
        bp = [_cmul(bbr, bbi, pr[k], pi[k]) for k in range(q)]
        bb = jnp.concatenate([bbr, -bbi], axis=1)
        if d == 0:
            vall = jnp.concatenate([jnp.concatenate(cp[l], axis=1) for l in range(q)], axis=0)
        else:
            vall = jnp.concatenate([jnp.concatenate(cp[q - 1 - l], axis=1) for l in range(q)], axis=0)
        mt = lax.dot_general(bb, vall, _NT, precision=lax.Precision.HIGHEST,
                             preferred_element_type=F32)
        blocks = []
        for s in range(q):
            if d == 0:
                n = gc * s
                blk = mt if n == 0 else jnp.where(lane >= n, pltpu.roll(mt, n, axis=1), 0.0)
            else:
                n = gc * (q - 1 - s)
                blk = mt if n == 0 else jnp.where(lane < w - n, pltpu.roll(mt, w - n, axis=1), 0.0)
            blocks.append(blk)
        kd = jnp.concatenate(blocks, axis=0)
        kmat = kd if kmat is None else kmat + kd
        zs = [bp[q - 1 - s] if d == 0 else bp[s] for s in range(q)]
        bz.append(jnp.concatenate([jnp.concatenate([zr, zi], axis=1) for zr, zi in zs], axis=0))
        bz.append(jnp.concatenate([jnp.concatenate([zi, zr], axis=1) for zr, zi in zs], axis=0))
        cs = [cp[t + 1] if d == 0 else cp[q - t] for t in range(q)]
        cmt.append(jnp.concatenate([jnp.concatenate([vr, -vi], axis=1) for vr, vi in cs], axis=0))
        coef_rows.append(jnp.concatenate([pr[q], pr[q]], axis=1))
        coef_rows.append(jnp.concatenate([-pi[q], pi[q]], axis=1))
    kc_ref[0:w, :] = kmat.astype(kc_ref.dtype)
    cm = jnp.concatenate(cmt, axis=1).T
    kc_ref[w:w + 4 * p, :] = cm.astype(kc_ref.dtype)
    bz_ref[...] = jnp.concatenate(bz, axis=1).astype(bz_ref.dtype)
    coef_rows += [jnp.zeros_like(coef_rows[0])] * 4
    coef_ref[...] = jnp.concatenate(coef_rows, axis=0)


def _s5_params(a_re, a_im, log_dt, b_re, b_im, c_re, c_im):
    _, g, p = a_re.shape
    gc = c_re.shape[1]
    q = SSM_CHUNK
    w = q * gc
    assert 2 * p == LANES
    dir_spec = pl.BlockSpec((2, None, 1, p), lambda i: (0, i, 0, 0))
    mat_spec = pl.BlockSpec((None, gc, p), lambda i: (i, 0, 0))
    ldt = jnp.broadcast_to(log_dt[:, :, None, None], (2, g, 1, p))
    return _call(
        functools.partial(_s5_param_kernel, q=q), grid=(g,),
        in_specs=[dir_spec, dir_spec, dir_spec, mat_spec, mat_spec, mat_spec, mat_spec],
        out_specs=[pl.BlockSpec((None, w + 4 * p, w), lambda i: (i, 0, 0)),
                   pl.BlockSpec((None, w, 8 * p), lambda i: (i, 0, 0)),
                   pl.BlockSpec((None, 8, 2 * p), lambda i: (i, 0, 0))],
        out_shape=[jax.ShapeDtypeStruct((g, w + 4 * p, w), BF16),
                   jax.ShapeDtypeStruct((g, w, 8 * p), BF16),
                   jax.ShapeDtypeStruct((g, 8, 2 * p), F32)], name="s5_params",
    )(a_re.reshape(2, g, 1, p), a_im.reshape(2, g, 1, p), ldt,
      jnp.swapaxes(b_re, 1, 2), jnp.swapaxes(b_im, 1, 2), c_re, c_im)


def _s5_scan_kernel(u_ref, kc_ref, bz_ref, coef_ref, y_ref, *scr, nb, nsub, nsub_ctx):
    ng, nrow, w = u_ref.shape
    z_scr, s_scr = scr[:4], scr[4:]
    for g in range(ng):
        z = jnp.dot(u_ref[g], bz_ref[g], preferred_element_type=F32)
        for k in range(4):
            z_scr[k][pl.ds(g, nrow, stride=ng), :] = z[:, k * LANES:(k + 1) * LANES]
    coef = [jnp.concatenate([coef_ref[g, k:k + 1, :] for g in range(ng)], axis=0) for k in range(4)]

    def body(n, carry):
        jb = jnp.where(n < nsub_ctx, nsub_ctx - 1 - n, nsub + nsub_ctx - 1 - n)
        new = []
        k = 0
        for b in range(nb):
            for d, j in ((0, n), (1, jb)):
                h, hs = carry[k], carry[k + 1]
                k += 2
                base = pl.multiple_of((b * nsub + j) * ng, ng)
                s_scr[d][pl.ds(base, ng), :] = h
                p1, p2 = coef[2 * d], coef[2 * d + 1]
                new.append(p1 * h + p2 * hs + z_scr[2 * d][pl.ds(base, ng), :])
                new.append(p1 * hs - p2 * h + z_scr[2 * d + 1][pl.ds(base, ng), :])
        return tuple(new)

    zero = jnp.zeros((ng, LANES), F32)
    lax.fori_loop(0, nsub, body, (zero,) * (4 * nb), unroll=4)
    for g in range(ng):
        u = u_ref[g]
        st = jnp.concatenate([s_scr[d][pl.ds(g, nrow, stride=ng), :] for d in range(2)], axis=1)
        y = jnp.dot(u, kc_ref[g, 0:w, :], preferred_element_type=F32)
        y += jnp.dot(st.astype(u.dtype), kc_ref[g, w:w + 2 * LANES, :], preferred_element_type=F32)
        y_ref[g] = y.astype(y_ref.dtype)


def _group_lane_masks(shape, gc):
    lane_group = lax.broadcasted_iota(jnp.int32, shape, 1) // gc
    return [lane_group == g for g in range(LANES // gc)]


def _s5_pack_kernel(x_ref, sh_ref, sc_ref, h_ref, u_ref, h_scr, *, q, gc):
    h = x_ref[...] * (1.0 + sc_ref[...]) + sh_ref[...]
    h_ref[...] = h.astype(h_ref.dtype)
    tm, d = h.shape
    nr = tm // q
    gpb = LANES // gc
    masks = _group_lane_masks((nr, LANES), gc)
    for blk in range(d // LANES):
        h_scr[blk] = h[:, blk * LANES:(blk + 1) * LANES]
        acc = [[None] * (q // gpb) for _ in range(gpb)]
        for s in range(q):
            xs = h_scr[blk, pl.ds(s, nr, stride=q), :]
            sb, col = s % gpb, s // gpb
            for dl in range(gpb):
                g = (sb - dl) % gpb
                moved = xs if dl == 0 else pltpu.roll(xs, dl * gc, axis=1)
                prev = acc[g][col]
                acc[g][col] = jnp.where(masks[sb], moved, 0.0 if prev is None else prev)
        for g in range(gpb):
            u_ref[blk * gpb + g] = jnp.concatenate(acc[g], axis=1).astype(u_ref.dtype)


def _s5_finish_kernel(y_ref, h_ref, d_ref, o_ref, y_scr, *, q, gc):
    nr = y_ref.shape[1]
    d = h_ref.shape[1]
    gpb = LANES // gc
    masks = _group_lane_masks((nr, LANES), gc)
    for blk in range(d // LANES):
        lanes = slice(blk * LANES, (blk + 1) * LANES)
        ys = [y_ref[blk * gpb + g].astype(F32) for g in range(gpb)]
        for s in range(q):
            sb, col = s % gpb, s // gpb
            xs = None
            for dl in range(gpb):
                g = (sb + dl) % gpb
                piece = ys[g][:, col * LANES:(col + 1) * LANES]
                moved = piece if dl == 0 else pltpu.roll(piece, dl * gc, axis=1)
                xs = jnp.where(masks[g], moved, 0.0 if xs is None else xs)
            y_scr[blk, pl.ds(s, nr, stride=q), :] = xs
        v = (y_scr[blk] + d_ref[:, lanes] * h_ref[:, lanes].astype(F32)).astype(h_ref.dtype)
        o_ref[:, lanes] = _gelu(v.astype(F32)).astype(o_ref.dtype)


def _s5_mixer(x, mods, rows, a_re, a_im, log_dt, b_re, b_im, c_re, c_im, d_skip, w_glu):
    t, d = x.shape
    g = a_re.shape[1]
    gc = d // g
    q = SSM_CHUNK
    w = q * gc
    r = t // q
    assert LANES % gc == 0 and w % LANES == 0 and q % (LANES // gc) == 0
    kc, bz, coef = _s5_params(a_re, a_im, log_dt, b_re, b_im, c_re, c_im)
    tm = ROW_TILE
    nr = tm // q
    gps = 8
    assert g % gps == 0
    row_spec = pl.BlockSpec((tm, d), lambda i: (i, 0))
    grp_spec = pl.BlockSpec((g, nr, w), lambda i: (0, i, 0))
    h, u = _call(
        functools.partial(_s5_pack_kernel, q=q, gc=gc), grid=(t // tm,),
        in_specs=[row_spec, _mod_spec(rows, tm, d, 0), _mod_spec(rows, tm, d, 1)],
        out_specs=[row_spec, grp_spec],
        out_shape=[jax.ShapeDtypeStruct((t, d), BF16), jax.ShapeDtypeStruct((g, r, w), BF16)],
        scratch_shapes=[pltpu.VMEM((d // LANES, tm, LANES), F32)], name="s5_pack",
    )(x, mods, mods)
    y = _call(
        functools.partial(_s5_scan_kernel, nb=rows.batch, nsub=rows.seg // q, nsub_ctx=rows.ctx // q),
        grid=(g // gps,),
        in_specs=[pl.BlockSpec((gps, r, w), lambda i: (i, 0, 0)),
                  pl.BlockSpec((gps,) + kc.shape[1:], lambda i: (i, 0, 0)),
                  pl.BlockSpec((gps,) + bz.shape[1:], lambda i: (i, 0, 0)),
                  pl.BlockSpec((gps,) + coef.shape[1:], lambda i: (i, 0, 0))],
        out_specs=pl.BlockSpec((gps, r, w), lambda i: (i, 0, 0)),
        out_shape=jax.ShapeDtypeStruct((g, r, w), BF16),
        scratch_shapes=[pltpu.VMEM((r * gps, LANES), F32)] * 6,
        name="s5_scan",
    )(u, kc, bz, coef)
    v = _call(
        functools.partial(_s5_finish_kernel, q=q, gc=gc), grid=(t // tm,),
        in_specs=[grp_spec, row_spec, pl.BlockSpec((1, d), lambda i: (0, 0))],
        out_specs=row_spec, out_shape=jax.ShapeDtypeStruct((t, d), BF16),
        scratch_shapes=[pltpu.VMEM((d // LANES, tm, LANES), F32)], name="s5_finish",
    )(y, h, d_skip.reshape(1, d))
    return _mm(v, w_glu, d, 256, mode="glu", second_col=d)


def _rope_tables(rows):
    half = LANES // 2
    quarter = half // 2
    tpos = jnp.arange(rows.lat)
    pos = jnp.stack([tpos // GRID_W, tpos % GRID_W], axis=1).astype(F32)
    inv = ROPE_THETA ** (-jnp.arange(quarter, dtype=F32) * 2.0 / half)
    ang = pos[:, :, None] * inv[None, None, :]
    ang = jnp.repeat(ang[:, :, None, :], 2, axis=2).reshape(rows.lat, LANES)
    sign = jnp.tile(jnp.repeat(jnp.array([-1.0, 1.0], F32), quarter), 2)
    cos = jnp.cos(ang)
    sin = jnp.sin(ang) * sign
    cos = jnp.concatenate([jnp.ones((rows.ctx, LANES), F32), cos], axis=0)
    sin = jnp.concatenate([jnp.zeros((rows.ctx, LANES), F32), sin], axis=0)
    return jnp.tile(cos, (rows.batch, 1)), jnp.tile(sin, (rows.batch, 1))


def _da_kernel(lam_ref, g_ref, q_ref, k_ref, v_ref, o_ref, s_scr, e_scr, *, lam_init, eps, tk):
    lp = lam_ref[...]
    dh = lp.shape[1]
    lam = (jnp.exp(jnp.sum(lp[0:1] * lp[1:2], axis=-1, keepdims=True))
           - jnp.exp(jnp.sum(lp[2:3] * lp[3:4], axis=-1, keepdims=True)) + lam_init)
    tq = q_ref.shape[0]
    nchunk = k_ref.shape[0] // tk
    v = v_ref[...]
    row_max = []
    for i in range(2):
        qi = q_ref[:, i * dh:(i + 1) * dh]
        m = jnp.full((tq, LANES), -jnp.inf, F32)
        for c in range(nchunk):
            s = lax.dot_general(qi, k_ref[c * tk:(c + 1) * tk, i * dh:(i + 1) * dh], _NT,
                                preferred_element_type=F32)
            s_scr[i, :, c * tk:(c + 1) * tk] = s
            for j in range(tk // LANES):
                m = jnp.maximum(m, s[:, j * LANES:(j + 1) * LANES])
        row_max.append(jnp.max(m, axis=-1, keepdims=True))
    outs = []
    for i in range(2):
        mrow = row_max[i]
        den = jnp.zeros((tq, LANES), F32)
        for c in range(nchunk):
            e = jnp.exp2(s_scr[i, :, c * tk:(c + 1) * tk] - mrow)
            for j in range(tk // LANES):
                den = den + e[:, j * LANES:(j + 1) * LANES]
            e_scr[i, :, c * tk:(c + 1) * tk] = e.astype(e_scr.dtype)
        den = jnp.sum(den, axis=-1, keepdims=True)
        outs.append(jnp.dot(e_scr[i], v, preferred_element_type=F32) * (1.0 / den))
    o = outs[0] - lam * outs[1]
    on = o * lax.rsqrt(jnp.mean(o * o, axis=-1, keepdims=True) + eps)
    o_ref[...] = (on * g_ref[...] * (1.0 - lam_init)).astype(o_ref.dtype)


def _diff_attention(h, rows, w_qkv, w_o, lam_p, subln_g, lam_init):
    t, d = h.shape
    dh = lam_p.shape[1]
    hw = 2 * dh
    heads = d // hw
    cos, sin = _rope_tables(rows)
    qk = _mm(h, w_qkv, 2 * d, 512, mode="rope", rope=(cos, sin), q_cols=d, q_scale=dh ** -0.5 * LOG2E)
    v = _mm(h, w_qkv, d, 512, col0=2 * d)
    tk = ROW_TILE
    body = functools.partial(_da_kernel, lam_init=lam_init, eps=LN_EPS, tk=tk)

    def scratch(nq, nk):
        return [pltpu.VMEM((2, nq, nk), F32), pltpu.VMEM((2, nq, nk), BF16)]

    lam_spec = pl.BlockSpec(lam_p.shape, lambda b, hd, i: (0, 0))
    g_spec = pl.BlockSpec((1, hw), lambda b, hd, i: (0, 0))
    tq = ROW_TILE
    s, c = rows.seg, rows.ctx
    o = _call(
        body, grid=(rows.batch, heads, rows.lat // tq),
        in_specs=[lam_spec, g_spec,
                  pl.BlockSpec((tq, hw), lambda b, hd, i: ((b * s + c) // tq + i, hd)),
                  pl.BlockSpec((s, hw), lambda b, hd, i: (b, heads + hd)),
                  pl.BlockSpec((s, hw), lambda b, hd, i: (b, hd))],
        out_specs=pl.BlockSpec((tq, hw), lambda b, hd, i: ((b * s + c) // tq + i, hd)),
        out_shape=jax.ShapeDtypeStruct((t, d), BF16), scratch_shapes=scratch(tq, s), name="diff_attn_lat",
    )(lam_p, subln_g.reshape(1, hw), qk, qk, v)
    o = _call(
        lambda lam_ref, g_ref, q_ref, k_ref, v_ref, prev_ref, o_ref, s_scr, e_scr: body(
            lam_ref, g_ref, q_ref, k_ref, v_ref, o_ref, s_scr, e_scr),
        grid=(rows.batch, heads, 1),
        in_specs=[lam_spec, g_spec,
                  pl.BlockSpec((c, hw), lambda b, hd, i: (b * s // c, hd)),
                  pl.BlockSpec((c, hw), lambda b, hd, i: (b * s // c, heads + hd)),
                  pl.BlockSpec((c, hw), lambda b, hd, i: (b * s // c, hd)),
                  pl.BlockSpec(memory_space=pl.ANY)],
        out_specs=pl.BlockSpec((c, hw), lambda b, hd, i: (b * s // c, hd)),
        out_shape=jax.ShapeDtypeStruct((t, d), BF16), aliases={5: 0}, scratch_shapes=scratch(c, c),
        name="diff_attn_ctx",
    )(lam_p, subln_g.reshape(1, hw), qk, qk, v, o)
    return _mm(o, w_o, d, 512)


def _cm_kernel(u_ref, v_ref, g_ref, b_ref, ws_ref, bs_ref, o_ref, *, eps):
    v = v_ref[...].astype(F32)
    mu = jnp.mean(v, axis=-1, keepdims=True)
    dv = v - mu
    var = jnp.mean(dv * dv, axis=-1, keepdims=True)
    vn = (dv * lax.rsqrt(var + eps) * g_ref[...] + b_ref[...]).astype(ws_ref.dtype)
    groups = ws_ref.shape[0]
    gw = vn.shape[1] // groups
    for g in range(groups):
        cols = slice(g * gw, (g + 1) * gw)
        vm = jnp.dot(ws_ref[g], vn[:, cols], preferred_element_type=F32) + bs_ref[:, g:g + 1]
        o_ref[:, cols] = (u_ref[:, cols].astype(F32) * vm).astype(o_ref.dtype)


def _chunk_mlp(h, w_in, ln_g, ln_b, w_s, b_s, w_out):
    t, d = h.shape
    uv = _mm(h, w_in, 2 * d, 512, mode="gelu")
    tc = CM_CHUNK
    groups = w_s.shape[0]
    vec_spec = pl.BlockSpec((1, d), lambda i: (0, 0))
    gated = _call(
        functools.partial(_cm_kernel, eps=LN_EPS), grid=(t // tc,),
        in_specs=[pl.BlockSpec((tc, d), lambda i: (i, 0)), pl.BlockSpec((tc, d), lambda i: (i, 1)),
                  vec_spec, vec_spec,
                  pl.BlockSpec((groups, tc, tc), lambda i: (0, 0, 0)),
                  pl.BlockSpec((tc, groups), lambda i: (0, 0))],
        out_specs=pl.BlockSpec((tc, d), lambda i: (i, 0)),
        out_shape=jax.ShapeDtypeStruct((t, d), BF16), name="chunk_gate",
    )(uv, uv, ln_g.reshape(1, d), ln_b.reshape(1, d), w_s.astype(BF16), b_s.T)
    return _mm(gated, w_out, d, 512)


def _na_bias_kernel(rpb_ref, o_ref, *, kh):
    cq = lax.broadcasted_iota(jnp.int32, (GRID_W, GRID_W), 0)
    ck = lax.broadcasted_iota(jnp.int32, (GRID_W, GRID_W), 1)
    cs = jnp.clip(cq - WIN_W // 2, 0, GRID_W - WIN_W)
    valid = jnp.logical_and(ck >= cs, ck < cs + WIN_W)
    tiles = []
    for dr in range(2 * WIN_H - 1):
        v = jnp.broadcast_to(rpb_ref[dr:dr + 1, :], (GRID_W, LANES)) * LOG2E
        t = pltpu.roll(v, LANES - (WIN_W - 1), axis=1, stride=1, stride_axis=0)[:, :GRID_W]
        tiles.append(jnp.where(valid, t, NEG_BIAS))
    for off in range(kh):
        o_ref[off] = jnp.concatenate([tiles[i - off + WIN_H - 1] for i in range(kh)], axis=1)


def _na_bias_table(rpb, kh):
    heads, nr, nc = rpb.shape
    assert nr <= 16 and nc <= LANES and GRID_W + WIN_W <= LANES
    padded = jnp.zeros((heads, 16, LANES), F32).at[:, :nr, :nc].set(rpb.astype(F32))
    return _call(
        functools.partial(_na_bias_kernel, kh=kh), grid=(heads,),
        in_specs=[pl.BlockSpec((None, 16, LANES), lambda hd: (hd, 0, 0))],
        out_specs=pl.BlockSpec((kh, None, GRID_W, kh * GRID_W), lambda hd: (0, hd, 0, 0)),
        out_shape=jax.ShapeDtypeStruct((kh, heads, GRID_W, kh * GRID_W), F32), name="nbr_bias",
    )(padded)


def _na_kernel(*refs, dh, kh, nrows, ctx, rq):
    bm_refs = refs[:rq]
    q_ref, k_ref, v_ref, o_ref = refs[rq:]
    nloc = kh * GRID_W
    chains = []
    for qi in range(rq):
        r = pl.program_id(2) * rq + qi
        rs = jnp.clip(r - kh // 2, 0, nrows - kh)
        start = pl.multiple_of(ctx + rs * GRID_W, GRID_W)
        qrows = slice(qi * GRID_W, (qi + 1) * GRID_W)
        for hd in range(q_ref.shape[1] // dh):
            cols = slice(hd * dh, (hd + 1) * dh)
            qh = q_ref[qrows, cols]
            sl = lax.dot_general(qh, k_ref[pl.ds(start, nloc), cols], _NT,
                                 preferred_element_type=F32) + bm_refs[qi][hd]
            sc = lax.dot_general(qh, k_ref[0:ctx, cols], _NT, preferred_element_type=F32)
            chains.append((qrows, cols, start, sl, sc))
    probs = []
    for _, _, _, sl, sc in chains:
        m = jnp.maximum(jnp.max(sl, axis=-1, keepdims=True), jnp.max(sc, axis=-1, keepdims=True))
        el = jnp.exp2(sl - m)
        ec = jnp.exp2(sc - m)
        den = jnp.sum(el, axis=-1, keepdims=True) + jnp.sum(ec, axis=-1, keepdims=True)
        probs.append((el.astype(v_ref.dtype), ec.astype(v_ref.dtype), 1.0 / den))
    for (qrows, cols, start, _, _), (el, ec, inv) in zip(chains, probs):
        o = (jnp.dot(el, v_ref[pl.ds(start, nloc), cols], preferred_element_type=F32)
             + jnp.dot(ec, v_ref[0:ctx, cols], preferred_element_type=F32))
        o_ref[qrows, cols] = (o * inv).astype(o_ref.dtype)


def _neighbourhood_attention(h, rows, w_qkv, w_o, rpb):
    t, d = h.shape
    heads = rpb.shape[0]
    dh = d // heads
    nrows = rows.lat // GRID_W
    kh = min(WIN_H, nrows)
    qkv = _mm(h, w_qkv, 3 * d, 512, q_cols=d, q_scale=dh ** -0.5 * LOG2E)
    bm = _na_bias_table(rpb, kh)
    hg = 4
    rq = 4
    hw = hg * dh
    s, c = rows.seg, rows.ctx
    assert nrows % rq == 0

    def bm_spec(qi):
        def index(b, g, r):
            row = r * rq + qi
            return (row - jnp.clip(row - kh // 2, 0, nrows - kh), g, 0, 0)
        return pl.BlockSpec((None, hg, GRID_W, kh * GRID_W), index)

    o = _call(
        functools.partial(_na_kernel, dh=dh, kh=kh, nrows=nrows, ctx=c, rq=rq),
        grid=(rows.batch, heads // hg, nrows // rq),
        in_specs=[bm_spec(qi) for qi in range(rq)] + [
            pl.BlockSpec((rq * GRID_W, hw), lambda b, g, r: ((b * s + c) // (rq * GRID_W) + r, g)),
            pl.BlockSpec((s, hw), lambda b, g, r: (b, d // hw + g)),
            pl.BlockSpec((s, hw), lambda b, g, r: (b, 2 * d // hw + g))],
        out_specs=pl.BlockSpec((rq * GRID_W, hw), lambda b, g, r: (b * (nrows // rq) + r, g)),
        out_shape=jax.ShapeDtypeStruct((rows.batch * rows.lat, d), BF16), name="nbr_attn",
    )(*([bm] * rq), qkv, qkv, qkv)
    return _mm(o, w_o, d, 512)


def kernel(x, c, ctx, c_ctx, ada_down, ada_up, ada_b, ln_g, ln_b, ffn_w_in, ffn_w_out,
           ssm_a_re, ssm_a_im, ssm_log_dt, ssm_b_re, ssm_b_im, ssm_c_re, ssm_c_im, ssm_d,
           ssm_w_glu, da_w_qkv, da_w_o, da_lambda, da_subln_g,
           cm_w_in, cm_ln_g, cm_ln_b, cm_w_s, cm_b_s, cm_w_out,
           na_w_qkv, na_w_o, na_rpb):
    bsz, seq, d = x.shape
    nctx = ctx.shape[1]
    depth = ada_down.shape[0]
    assert depth == 4 and bsz + 1 <= 8, "one layer per mixer kind; the last layer drops the context stream"
    assert nctx % ROW_TILE == 0 and seq % ROW_TILE == 0 and seq % GRID_W == 0
    alpha = (2 * depth) ** 0.25

    cond = jnp.zeros((8, d), F32).at[:bsz].set(c).at[bsz].set(c_ctx)
    mods = _adaln(cond, ada_down, ada_up, ada_b).reshape(depth, 8, 1, 6 * d)
    uni = _Rows(bsz, nctx, seq, True)
    lat = _Rows(bsz, nctx, seq, False)

    xs = jnp.concatenate([ctx, x], axis=1).reshape(uni.total, d)
    w_out_bf16 = ffn_w_out.astype(BF16)
    h = None
    for i in range(depth):
        rows = uni
        if i == 0:
            y = _s5_mixer(xs, mods[0], uni, ssm_a_re[0], ssm_a_im[0], ssm_log_dt[0], ssm_b_re[0], ssm_b_im[0],
                          ssm_c_re[0], ssm_c_im[0], ssm_d[0], ssm_w_glu[0])
        elif i == 1:
            y = _diff_attention(h, uni, da_w_qkv[0], da_w_o[0], da_lambda[0], da_subln_g[0],
                                0.8 - 0.6 * math.exp(-0.3 * i))
        elif i == 2:
            y = _chunk_mlp(h, cm_w_in[0], cm_ln_g[0], cm_ln_b[0], cm_w_s[0], cm_b_s[0], cm_w_out[0])
        else:
            y = _neighbourhood_attention(h, uni, na_w_qkv[0], na_w_o[0], na_rpb[0])
            rows = lat
        xs, h = _ln_res(xs, uni, y, rows, mods[i], 2, ln_g[i, 0], ln_b[i, 0], alpha,
                        next_mods=mods[i], next_chunks=(3, 4))
        y = _ffn(h, ffn_w_in, w_out_bf16, i)
        if i + 1 < depth:
            xs, h = _ln_res(xs, rows, y, rows, mods[i], 5, ln_g[i, 1], ln_b[i, 1], alpha,
                            next_mods=mods[i + 1], next_chunks=(0, 1))
        else:
            xs, _ = _ln_res(xs, rows, y, rows, mods[i], 5, ln_g[i, 1], ln_b[i, 1], alpha)
    return xs.reshape(bsz, seq, d)
```

```python
import functools
import math
from typing import NamedTuple

import jax
import jax.numpy as jnp
from jax import lax
from jax.experimental import pallas as pl
from jax.experimental.pallas import tpu as pltpu

F32 = jnp.float32
BF16 = jnp.bfloat16

GRID_W = 64
LN_EPS = 1e-5
SSM_CHUNK = 16
ROPE_THETA = 10000.0
CM_CHUNK = 128
WIN_H = 8
WIN_W = 16
NEG_BIAS = -1e30
LOG2E = 1.4426950408889634

LANES = 128
ROW_TILE = 256
EW_ROW_TILE = 256
VMEM_LIMIT_MB = 56

_NT = (((1,), (1,)), ((), ()))


class _Rows(NamedTuple):
    batch: int
    ctx: int
    lat: int
    with_ctx: bool

    @property
    def seg(self):
        return (self.ctx if self.with_ctx else 0) + self.lat

    @property
    def total(self):
        return self.batch * self.seg

    def mod_row(self, i, tm):
        per = self.seg // tm
        b = i // per
        if not self.with_ctx:
            return b
        return jnp.where(i % per < self.ctx // tm, self.batch, b)

    def tile_in(self, other, i, tm):
        if other.with_ctx == self.with_ctx:
            return i
        assert other.with_ctx and not self.with_ctx
        return i + (i // (self.lat // tm) + 1) * (self.ctx // tm)


def _call(kernel, *, grid, in_specs, out_specs, out_shape, scratch_shapes=(), aliases=None, name=None):
    return pl.pallas_call(
        kernel, grid=grid, in_specs=in_specs, out_specs=out_specs, out_shape=out_shape,
        scratch_shapes=list(scratch_shapes), input_output_aliases=aliases or {}, name=name,
        compiler_params=pltpu.CompilerParams(
            dimension_semantics=("arbitrary",) * len(grid),
            vmem_limit_bytes=VMEM_LIMIT_MB << 20))


def _gelu(x):
    return 0.5 * x * (1.0 + lax.erf(x * (2.0 ** -0.5)))


def _adaln_kernel(c_ref, wd_ref, wu_ref, b_ref, o_ref, t_scr):
    @pl.when(pl.program_id(1) == 0)
    def _():
        c = c_ref[...]
        t_scr[...] = jnp.dot(c * jax.nn.sigmoid(c), wd_ref[...], precision=lax.Precision.HIGHEST,
                             preferred_element_type=F32)

    o_ref[...] = jnp.dot(t_scr[...], wu_ref[...], precision=lax.Precision.HIGHEST,
                         preferred_element_type=F32) + b_ref[...]


def _adaln(cond, w_down, w_up, b):
    depth, d, r = w_down.shape
    n = w_up.shape[2]
    tn = d
    return _call(
        _adaln_kernel, grid=(depth, n // tn),
        in_specs=[pl.BlockSpec((8, d), lambda l, j: (0, 0)),
                  pl.BlockSpec((None, d, r), lambda l, j: (l, 0, 0)),
                  pl.BlockSpec((None, r, tn), lambda l, j: (l, 0, j)),
                  pl.BlockSpec((None, 1, tn), lambda l, j: (l, 0, j))],
        out_specs=pl.BlockSpec((None, 8, tn), lambda l, j: (l, 0, j)),
        out_shape=jax.ShapeDtypeStruct((depth, 8, n), F32),
        scratch_shapes=[pltpu.VMEM((8, r), F32)], name="adaln",
    )(cond, w_down, w_up, b.reshape(depth, 1, n))


def _mod_spec(rows, tm, d, chunk):
    return pl.BlockSpec((None, 1, d), lambda i: (rows.mod_row(i, tm), 0, chunk))


def _ln_res_kernel(*refs, alpha, eps, has_next):
    if has_next:
        x_ref, y_ref, gate_ref, g_ref, b_ref, sh_ref, sc_ref, xo_ref, ho_ref = refs
    else:
        x_ref, y_ref, gate_ref, g_ref, b_ref, xo_ref = refs
    z = alpha * x_ref[...] + gate_ref[...] * y_ref[...].astype(F32)
    mu = jnp.mean(z, axis=-1, keepdims=True)
    dz = z - mu
    var = jnp.mean(dz * dz, axis=-1, keepdims=True)
    xn = dz * lax.rsqrt(var + eps) * g_ref[...] + b_ref[...]
    xo_ref[...] = xn
    if has_next:
        ho_ref[...] = (xn * (1.0 + sc_ref[...]) + sh_ref[...]).astype(ho_ref.dtype)


def _ln_res(x, x_rows, y, rows, gate_mods, gate_chunk, ln_g, ln_b, alpha, next_mods=None, next_chunks=None):
    t, d = y.shape
    tm = EW_ROW_TILE
    has_next = next_mods is not None
    row_spec = pl.BlockSpec((tm, d), lambda i: (i, 0))
    vec_spec = pl.BlockSpec((1, d), lambda i: (0, 0))
    in_specs = [pl.BlockSpec((tm, d), lambda i: (rows.tile_in(x_rows, i, tm), 0)), row_spec,
                _mod_spec(rows, tm, d, gate_chunk), vec_spec, vec_spec]
    args = [x, y, gate_mods, ln_g.reshape(1, d), ln_b.reshape(1, d)]
    out_specs = [row_spec]
    out_shape = [jax.ShapeDtypeStruct((t, d), F32)]
    if has_next:
        in_specs += [_mod_spec(rows, tm, d, next_chunks[0]), _mod_spec(rows, tm, d, next_chunks[1])]
        args += [next_mods, next_mods]
        out_specs.append(row_spec)
        out_shape.append(jax.ShapeDtypeStruct((t, d), BF16))
    out = _call(
        functools.partial(_ln_res_kernel, alpha=alpha, eps=LN_EPS, has_next=has_next),
        grid=(t // tm,), in_specs=in_specs, out_specs=out_specs, out_shape=out_shape, name="ln_res",
    )(*args)
    return (out[0], out[1]) if has_next else (out[0], None)


def _swap32(a):
    lane = lax.broadcasted_iota(jnp.int32, a.shape, 1)
    return jnp.where(lane % 64 < 32, pltpu.roll(a, LANES - 32, axis=1), pltpu.roll(a, 32, axis=1))


def _mm_kernel(*refs, mode, nw, cast, q_tiles, q_scale):
    w_refs = refs[1:1 + nw]
    o_ref = refs[-1 - (nw if cast else 0)]
    if cast:
        scr = refs[len(refs) - nw:]

        @pl.when(pl.program_id(1) == 0)
        def _():
            for w_ref, s_ref in zip(w_refs, scr):
                s_ref[...] = w_ref[...].astype(s_ref.dtype)

        w_refs = scr
    x = refs[0][...]
    a = jnp.dot(x, w_refs[0][...], preferred_element_type=F32)
    if mode == "plain":
        out = a
    elif mode == "gelu":
        out = _gelu(a)
    elif mode == "swiglu":
        b = jnp.dot(x, w_refs[1][...], preferred_element_type=F32)
        out = a * jax.nn.sigmoid(a) * b
    elif mode == "glu":
        b = jnp.dot(x, w_refs[1][...], preferred_element_type=F32)
        out = a * jax.nn.sigmoid(b)
    elif mode == "rope":
        cos = refs[1 + nw][...]
        sin = refs[2 + nw][...]
        pieces = []
        for g in range(a.shape[1] // LANES):
            ag = a[:, g * LANES:(g + 1) * LANES]
            pieces.append(ag * cos + _swap32(ag) * sin)
        out = jnp.concatenate(pieces, axis=1)
    else:
        raise ValueError(mode)
    if q_tiles:
        out = out * jnp.where(pl.program_id(0) < q_tiles, q_scale, 1.0)
    o_ref[...] = out.astype(o_ref.dtype)


def _row_tile(t, k):
    budget = (24 << 20) if k <= 8192 else (12 << 20)
    for nt in range(1, t // 16 + 1):
        tm = t // nt
        if t % nt == 0 and tm % 16 == 0 and 4 * tm * k <= budget:
            return tm
    raise ValueError((t, k))


def _mm(x, w, n_out, tn, *, mode="plain", layer=None, col0=0, second_col=None, rope=None, q_cols=0,
        q_scale=1.0, out_dtype=BF16):
    t, k = x.shape
    tm = _row_tile(t, k)
    assert n_out % tn == 0 and col0 % tn == 0 and q_cols % tn == 0
    cast = w.dtype != BF16
    offs = [col0 // tn] + ([second_col // tn] if mode in ("swiglu", "glu") else [])
    in_specs = [pl.BlockSpec((tm, k), lambda j, i: (i, 0))]
    if layer is None:
        in_specs += [pl.BlockSpec((k, tn), lambda j, i, off=off: (0, j + off)) for off in offs]
    else:
        in_specs += [pl.BlockSpec((None, k, tn), lambda j, i, off=off: (layer, 0, j + off)) for off in offs]
    args = [x] + [w] * len(offs)
    if mode == "rope":
        in_specs += [pl.BlockSpec((tm, LANES), lambda j, i: (i, 0))] * 2
        args += list(rope)
    return _call(
        functools.partial(_mm_kernel, mode=mode, nw=len(offs), cast=cast, q_tiles=q_cols // tn, q_scale=q_scale),
        grid=(n_out // tn, t // tm),
        in_specs=in_specs, out_specs=pl.BlockSpec((tm, tn), lambda j, i: (i, j)),
        out_shape=jax.ShapeDtypeStruct((t, n_out), out_dtype),
        scratch_shapes=[pltpu.VMEM((k, tn), BF16)] * len(offs) if cast else (), name="mm_" + mode,
    )(*args)


def _ffn(h, w_in, w_out, layer):
    f = w_out.shape[1]
    a = _mm(h, w_in, f, 256, mode="swiglu", layer=layer, second_col=f)
    return _mm(a, w_out, w_out.shape[2], 512, layer=layer)


def _cmul(ar, ai, br, bi):
    return ar * br - ai * bi, ar * bi + ai * br


def _s5_param_kernel(are_ref, aim_ref, ldt_ref, btr_ref, bti_ref, cre_ref, cim_ref,
                     kc_ref, bz_ref, coef_ref, *, q):
    for g in range(cre_ref.shape[0]):
        _s5_param_group(are_ref.at[:, g], aim_ref.at[:, g], ldt_ref.at[:, g], btr_ref.at[g], bti_ref.at[g],
                        cre_ref.at[g], cim_ref.at[g], kc_ref.at[g], bz_ref.at[g], coef_ref.at[g], q=q)


def _s5_param_group(are_ref, aim_ref, ldt_ref, btr_ref, bti_ref, cre_ref, cim_ref,
                    kc_ref, bz_ref, coef_ref, *, q):
    cre = cre_ref[...]
    cim = cim_ref[...]
    gc, p = cre.shape
    w = q * gc
    lane = lax.broadcasted_iota(jnp.int32, (gc, w), 1)
    kmat = None
    cmt = []
    bz = []
    coef_rows = []
    for d in range(2):
        lr = are_ref[d]
        li = aim_ref[d]
        dt = jnp.exp(ldt_ref[d])
        ea = jnp.exp(lr * dt)
        ar = ea * jnp.cos(li * dt)
        ai = ea * jnp.sin(li * dt)
        den = lr * lr + li * li
        nr = ar - 1.0
        kr = (nr * lr + ai * li) / den
        ki = (ai * lr - nr * li) / den
        bbr, bbi = _cmul(kr, ki, btr_ref[...], bti_ref[...])
        pr = [jnp.ones_like(ar)]
        pi = [jnp.zeros_like(ar)]
        for _ in range(q):
            nr_, ni_ = _cmul(pr[-1], pi[-1], ar, ai)
            pr.append(nr_)
            pi.append(ni_)
        cp = [_cmul(cre, cim, pr[k], pi[k]) for k in range(q + 1)]
        bp = [_cmul(bbr, bbi, pr[k], pi[k]) for k in range(q)]
        bb = jnp.concatenate([bbr, -bbi], axis=1)
        if d == 0:
            vall = jnp.concatenate([jnp.concatenate(cp[l], axis=1) for l in range(q)], axis=0)
        else:
            vall = jnp.concatenate([jnp.concatenate(cp[q - 1 - l], axis=1) for l in range(q)], axis=0)
        mt = lax.dot_general(bb, vall, _NT, precision=lax.Precision.HIGHEST,
                             preferred_element_type=F32)
        blocks = []
        for s in range(q):
            if d == 0:
                n = gc * s
                blk = mt if n == 0 else jnp.where(lane >= n, pltpu.roll(mt, n, axis=1), 0.0)
            else:
                n = gc * (q - 1 - s)
                blk = mt if n == 0 else jnp.where(lane < w - n, pltpu.roll(mt, w - n, axis=1), 0.0)
            blocks.append(blk)
        kd = jnp.concatenate(blocks, axis=0)
        kmat = kd if kmat is None else kmat + kd
        zs = [bp[q - 1 - s] if d == 0 else bp[s] for s in range(q)]
        bz.append(jnp.concatenate([jnp.concatenate([zr, zi], axis=1) for zr, zi in zs], axis=0))
        bz.append(jnp.concatenate([jnp.concatenate([zi, zr], axis=1) for zr, zi in zs], axis=0))
        cs = [cp[t + 1] if d == 0 else cp[q - t] for t in range(q)]
        cmt.append(jnp.concatenate([jnp.concatenate([vr, -vi], axis=1) for vr, vi in cs], axis=0))
        coef_rows.append(jnp.concatenate([pr[q], pr[q]], axis=1))
        coef_rows.append(jnp.concatenate([-pi[q], pi[q]], axis=1))
    kc_ref[0:w, :] = kmat.astype(kc_ref.dtype)
    cm = jnp.concatenate(cmt, axis=1).T
    kc_ref[w:w + 4 * p, :] = cm.astype(kc_ref.dtype)
    bz_ref[...] = jnp.concatenate(bz, axis=1).astype(bz_ref.dtype)
    coef_rows += [jnp.zeros_like(coef_rows[0])] * 4
    coef_ref[...] = jnp.concatenate(coef_rows, axis=0)


def _s5_params(a_re, a_im, log_dt, b_re, b_im, c_re, c_im):
    _, g, p = a_re.shape
    gc = c_re.shape[1]
    q = SSM_CHUNK
    w = q * gc
    assert 2 * p == LANES
    ng = 4
    assert g % ng == 0
    dir_spec = pl.BlockSpec((2, ng, 1, p), lambda i: (0, i, 0, 0))
    mat_spec = pl.BlockSpec((ng, gc, p), lambda i: (i, 0, 0))
    ldt = jnp.broadcast_to(log_dt[:, :, None, None], (2, g, 1, p))
    return _call(
        functools.partial(_s5_param_kernel, q=q), grid=(g // ng,),
        in_specs=[dir_spec, dir_spec, dir_spec, mat_spec, mat_spec, mat_spec, mat_spec],
        out_specs=[pl.BlockSpec((ng, w + 4 * p, w), lambda i: (i, 0, 0)),
                   pl.BlockSpec((ng, w, 8 * p), lambda i: (i, 0, 0)),
                   pl.BlockSpec((ng, 8, 2 * p), lambda i: (i, 0, 0))],
        out_shape=[jax.ShapeDtypeStruct((g, w + 4 * p, w), BF16),
                   jax.ShapeDtypeStruct((g, w, 8 * p), BF16),
                   jax.ShapeDtypeStruct((g, 8, 2 * p), F32)], name="s5_params",
    )(a_re.reshape(2, g, 1, p), a_im.reshape(2, g, 1, p), ldt,
      jnp.swapaxes(b_re, 1, 2), jnp.swapaxes(b_im, 1, 2), c_re, c_im)


def _s5_scan_kernel(u_ref, kc_ref, bz_ref, coef_ref, y_ref, *scr, nb, nsub, nsub_ctx):
    ng, nrow, w = u_ref.shape
    z_scr, s_scr = scr[:4], scr[4:]
    for g in range(ng):
        z = jnp.dot(u_ref[g], bz_ref[g], preferred_element_type=F32)
        for k in range(4):
            z_scr[k][pl.ds(g, nrow, stride=ng), :] = z[:, k * LANES:(k + 1) * LANES]
    coef = [jnp.concatenate([coef_ref[g, k:k + 1, :] for g in range(ng)], axis=0) for k in range(4)]

    def body(n, carry):
        jb = jnp.where(n < nsub_ctx, nsub_ctx - 1 - n, nsub + nsub_ctx - 1 - n)
        new = []
        k = 0
        for b in range(nb):
            for d, j in ((0, n), (1, jb)):
                h, hs = carry[k], carry[k + 1]
                k += 2
                base = pl.multiple_of((b * nsub + j) * ng, ng)
                s_scr[d][pl.ds(base, ng), :] = h
                p1, p2 = coef[2 * d], coef[2 * d + 1]
                new.append(p1 * h + p2 * hs + z_scr[2 * d][pl.ds(base, ng), :])
                new.append(p1 * hs - p2 * h + z_scr[2 * d + 1][pl.ds(base, ng), :])
        return tuple(new)

    zero = jnp.zeros((ng, LANES), F32)
    lax.fori_loop(0, nsub, body, (zero,) * (4 * nb), unroll=4)
    for g in range(ng):
        u = u_ref[g]
        st = jnp.concatenate([s_scr[d][pl.ds(g, nrow, stride=ng), :] for d in range(2)], axis=1)
        y = jnp.dot(u, kc_ref[g, 0:w, :], preferred_element_type=F32)
        y += jnp.dot(st.astype(u.dtype), kc_ref[g, w:w + 2 * LANES, :], preferred_element_type=F32)
        y_ref[g] = y.astype(y_ref.dtype)


def _group_lane_masks(shape, gc):
    lane_group = lax.broadcasted_iota(jnp.int32, shape, 1) // gc
    return [lane_group == g for g in range(LANES // gc)]


def _s5_pack_kernel(x_ref, sh_ref, sc_ref, h_ref, u_ref, h_scr, *, q, gc):
    h = x_ref[...] * (1.0 + sc_ref[...]) + sh_ref[...]
    h_ref[...] = h.astype(h_ref.dtype)
    tm, d = h.shape
    nr = tm // q
    gpb = LANES // gc
    masks = _group_lane_masks((nr, LANES), gc)
    for blk in range(d // LANES):
        h_scr[blk] = h[:, blk * LANES:(blk + 1) * LANES]
        acc = [[None] * (q // gpb) for _ in range(gpb)]
        for s in range(q):
            xs = h_scr[blk, pl.ds(s, nr, stride=q), :]
            sb, col = s % gpb, s // gpb
            for dl in range(gpb):
                g = (sb - dl) % gpb
                moved = xs if dl == 0 else pltpu.roll(xs, dl * gc, axis=1)
                prev = acc[g][col]
                acc[g][col] = jnp.where(masks[sb], moved, 0.0 if prev is None else prev)
        for g in range(gpb):
            u_ref[blk * gpb + g] = jnp.concatenate(acc[g], axis=1).astype(u_ref.dtype)


def _s5_finish_kernel(y_ref, h_ref, d_ref, o_ref, y_scr, *, q, gc):
    nr = y_ref.shape[1]
    d = h_ref.shape[1]
    gpb = LANES // gc
    masks = _group_lane_masks((nr, LANES), gc)
    for blk in range(d // LANES):
        lanes = slice(blk * LANES, (blk + 1) * LANES)
        ys = [y_ref[blk * gpb + g].astype(F32) for g in range(gpb)]
        for s in range(q):
            sb, col = s % gpb, s // gpb
            xs = None
            for dl in range(gpb):
                g = (sb + dl) % gpb
                piece = ys[g][:, col * LANES:(col + 1) * LANES]
                moved = piece if dl == 0 else pltpu.roll(piece, dl * gc, axis=1)
                xs = jnp.where(masks[g], moved, 0.0 if xs is None else xs)
            y_scr[blk, pl.ds(s, nr, stride=q), :] = xs
        v = (y_scr[blk] + d_ref[:, lanes] * h_ref[:, lanes].astype(F32)).astype(h_ref.dtype)
        o_ref[:, lanes] = _gelu(v.astype(F32)).astype(o_ref.dtype)


def _s5_mixer(x, mods, rows, a_re, a_im, log_dt, b_re, b_im, c_re, c_im, d_skip, w_glu):
    t, d = x.shape
    g = a_re.shape[1]
    gc = d // g
    q = SSM_CHUNK
    w = q * gc
    r = t // q
    assert LANES % gc == 0 and w % LANES == 0 and q % (LANES // gc) == 0
    kc, bz, coef = _s5_params(a_re, a_im, log_dt, b_re, b_im, c_re, c_im)
    tm = ROW_TILE
    nr = tm // q
    gps = 8
    assert g % gps == 0
    row_spec = pl.BlockSpec((tm, d), lambda i: (i, 0))
    grp_spec = pl.BlockSpec((g, nr, w), lambda i: (0, i, 0))
    h, u = _call(
        functools.partial(_s5_pack_kernel, q=q, gc=gc), grid=(t // tm,),
        in_specs=[row_spec, _mod_spec(rows, tm, d, 0), _mod_spec(rows, tm, d, 1)],
        out_specs=[row_spec, grp_spec],
        out_shape=[jax.ShapeDtypeStruct((t, d), BF16), jax.ShapeDtypeStruct((g, r, w), BF16)],
        scratch_shapes=[pltpu.VMEM((d // LANES, tm, LANES), F32)], name="s5_pack",
    )(x, mods, mods)
    y = _call(
        functools.partial(_s5_scan_kernel, nb=rows.batch, nsub=rows.seg // q, nsub_ctx=rows.ctx // q),
        grid=(g // gps,),
        in_specs=[pl.BlockSpec((gps, r, w), lambda i: (i, 0, 0)),
                  pl.BlockSpec((gps,) + kc.shape[1:], lambda i: (i, 0, 0)),
                  pl.BlockSpec((gps,) + bz.shape[1:], lambda i: (i, 0, 0)),
                  pl.BlockSpec((gps,) + coef.shape[1:], lambda i: (i, 0, 0))],
        out_specs=pl.BlockSpec((gps, r, w), lambda i: (i, 0, 0)),
        out_shape=jax.ShapeDtypeStruct((g, r, w), BF16),
        scratch_shapes=[pltpu.VMEM((r * gps, LANES), F32)] * 6,
        name="s5_scan",
    )(u, kc, bz, coef)
    v = _call(
        functools.partial(_s5_finish_kernel, q=q, gc=gc), grid=(t // tm,),
        in_specs=[grp_spec, row_spec, pl.BlockSpec((1, d), lambda i: (0, 0))],
        out_specs=row_spec, out_shape=jax.ShapeDtypeStruct((t, d), BF16),
        scratch_shapes=[pltpu.VMEM((d // LANES, tm, LANES), F32)], name="s5_finish",
    )(y, h, d_skip.reshape(1, d))
    return _mm(v, w_glu, d, 256, mode="glu", second_col=d)


def _rope_tables(rows):
    half = LANES // 2
    quarter = half // 2
    tpos = jnp.arange(rows.lat)
    pos = jnp.stack([tpos // GRID_W, tpos % GRID_W], axis=1).astype(F32)
    inv = ROPE_THETA ** (-jnp.arange(quarter, dtype=F32) * 2.0 / half)
    ang = pos[:, :, None] * inv[None, None, :]
    ang = jnp.repeat(ang[:, :, None, :], 2, axis=2).reshape(rows.lat, LANES)
    sign = jnp.tile(jnp.repeat(jnp.array([-1.0, 1.0], F32), quarter), 2)
    cos = jnp.cos(ang)
    sin = jnp.sin(ang) * sign
    cos = jnp.concatenate([jnp.ones((rows.ctx, LANES), F32), cos], axis=0)
    sin = jnp.concatenate([jnp.zeros((rows.ctx, LANES), F32), sin], axis=0)
    return jnp.tile(cos, (rows.batch, 1)), jnp.tile(sin, (rows.batch, 1))


def _da_kernel(lam_ref, g_ref, q_ref, k_ref, v_ref, o_ref, s_scr, e_scr, *, lam_init, eps, tk):
    lp = lam_ref[...]
    dh = lp.shape[1]
    lam = (jnp.exp(jnp.sum(lp[0:1] * lp[1:2], axis=-1, keepdims=True))
           - jnp.exp(jnp.sum(lp[2:3] * lp[3:4], axis=-1, keepdims=True)) + lam_init)
    tq = q_ref.shape[0]
    nchunk = k_ref.shape[0] // tk
    nunit = q_ref.shape[1] // dh
    row_max = []
    for u in range(nunit):
        qu = q_ref[:, u * dh:(u + 1) * dh]
        m = jnp.full((tq, LANES), -jnp.inf, F32)
        for c in range(nchunk):
            s = lax.dot_general(qu, k_ref[c * tk:(c + 1) * tk, u * dh:(u + 1) * dh], _NT,
                                preferred_element_type=F32)
            s_scr[u, :, c * tk:(c + 1) * tk] = s
            for j in range(tk // LANES):
                m = jnp.maximum(m, s[:, j * LANES:(j + 1) * LANES])
        row_max.append(jnp.max(m, axis=-1, keepdims=True))
    outs = []
    for u in range(nunit):
        mrow = row_max[u]
        den = jnp.zeros((tq, LANES), F32)
        for c in range(nchunk):
            e = jnp.exp2(s_scr[u, :, c * tk:(c + 1) * tk] - mrow)
            for j in range(tk // LANES):
                den = den + e[:, j * LANES:(j + 1) * LANES]
            e_scr[u, :, c * tk:(c + 1) * tk] = e.astype(e_scr.dtype)
        den = jnp.sum(den, axis=-1, keepdims=True)
        hd = u // 2
        outs.append(jnp.dot(e_scr[u], v_ref[:, hd * 2 * dh:(hd + 1) * 2 * dh],
                            preferred_element_type=F32) * (1.0 / den))
    for hd in range(nunit // 2):
        o = outs[2 * hd] - lam * outs[2 * hd + 1]
        on = o * lax.rsqrt(jnp.mean(o * o, axis=-1, keepdims=True) + eps)
        o_ref[:, hd * 2 * dh:(hd + 1) * 2 * dh] = (on * g_ref[...] * (1.0 - lam_init)).astype(o_ref.dtype)


def _diff_attention(h, rows, w_qkv, w_o, lam_p, subln_g, lam_init):
    t, d = h.shape
    dh = lam_p.shape[1]
    hb = 2
    hw = hb * 2 * dh
    heads = d // hw
    cos, sin = _rope_tables(rows)
    qk = _mm(h, w_qkv, 2 * d, 512, mode="rope", rope=(cos, sin), q_cols=d, q_scale=dh ** -0.5 * LOG2E)
    v = _mm(h, w_qkv, d, 512, col0=2 * d)
    tk = ROW_TILE
    body = functools.partial(_da_kernel, lam_init=lam_init, eps=LN_EPS, tk=tk)

    def scratch(nq, nk):
        return [pltpu.VMEM((2 * hb, nq, nk), F32), pltpu.VMEM((2 * hb, nq, nk), BF16)]

    lam_spec = pl.BlockSpec(lam_p.shape, lambda b, hd, i: (0, 0))
    g_spec = pl.BlockSpec((1, 2 * dh), lambda b, hd, i: (0, 0))
    tq = ROW_TILE
    s, c = rows.seg, rows.ctx
    o = _call(
        body, grid=(rows.batch, heads, rows.lat // tq),
        in_specs=[lam_spec, g_spec,
                  pl.BlockSpec((tq, hw), lambda b, hd, i: ((b * s + c) // tq + i, hd)),
                  pl.BlockSpec((s, hw), lambda b, hd, i: (b, heads + hd)),
                  pl.BlockSpec((s, hw), lambda b, hd, i: (b, hd))],
        out_specs=pl.BlockSpec((tq, hw), lambda b, hd, i: ((b * s + c) // tq + i, hd)),
        out_shape=jax.ShapeDtypeStruct((t, d), BF16), scratch_shapes=scratch(tq, s), name="diff_attn_lat",
    )(lam_p, subln_g.reshape(1, 2 * dh), qk, qk, v)
    o = _call(
        lambda lam_ref, g_ref, q_ref, k_ref, v_ref, prev_ref, o_ref, s_scr, e_scr: body(
            lam_ref, g_ref, q_ref, k_ref, v_ref, o_ref, s_scr, e_scr),
        grid=(rows.batch, heads, 1),
        in_specs=[lam_spec, g_spec,
                  pl.BlockSpec((c, hw), lambda b, hd, i: (b * s // c, hd)),
                  pl.BlockSpec((c, hw), lambda b, hd, i: (b * s // c, heads + hd)),
                  pl.BlockSpec((c, hw), lambda b, hd, i: (b * s // c, hd)),
                  pl.BlockSpec(memory_space=pl.ANY)],
        out_specs=pl.BlockSpec((c, hw), lambda b, hd, i: (b * s // c, hd)),
        out_shape=jax.ShapeDtypeStruct((t, d), BF16), aliases={5: 0}, scratch_shapes=scratch(c, c),
        name="diff_attn_ctx",
    )(lam_p, subln_g.reshape(1, 2 * dh), qk, qk, v, o)
    return _mm(o, w_o, d, 512)


def _cm_kernel(u_ref, v_ref, g_ref, b_ref, ws_ref, bs_ref, o_ref, *, eps):
    v = v_ref[...].astype(F32)
    mu = jnp.mean(v, axis=-1, keepdims=True)
    dv = v - mu
    var = jnp.mean(dv * dv, axis=-1, keepdims=True)
    vn = (dv * lax.rsqrt(var + eps) * g_ref[...] + b_ref[...]).astype(ws_ref.dtype)
    groups = ws_ref.shape[0]
    gw = vn.shape[1] // groups
    for g in range(groups):
        cols = slice(g * gw, (g + 1) * gw)
        vm = jnp.dot(ws_ref[g], vn[:, cols], preferred_element_type=F32) + bs_ref[:, g:g + 1]
        o_ref[:, cols] = (u_ref[:, cols].astype(F32) * vm).astype(o_ref.dtype)


def _chunk_mlp(h, w_in, ln_g, ln_b, w_s, b_s, w_out):
    t, d = h.shape
    uv = _mm(h, w_in, 2 * d, 512, mode="gelu")
    tc = CM_CHUNK
    groups = w_s.shape[0]
    vec_spec = pl.BlockSpec((1, d), lambda i: (0, 0))
    gated = _call(
        functools.partial(_cm_kernel, eps=LN_EPS), grid=(t // tc,),
        in_specs=[pl.BlockSpec((tc, d), lambda i: (i, 0)), pl.BlockSpec((tc, d), lambda i: (i, 1)),
                  vec_spec, vec_spec,
                  pl.BlockSpec((groups, tc, tc), lambda i: (0, 0, 0)),
                  pl.BlockSpec((tc, groups), lambda i: (0, 0))],
        out_specs=pl.BlockSpec((tc, d), lambda i: (i, 0)),
        out_shape=jax.ShapeDtypeStruct((t, d), BF16), name="chunk_gate",
    )(uv, uv, ln_g.reshape(1, d), ln_b.reshape(1, d), w_s.astype(BF16), b_s.T)
    return _mm(gated, w_out, d, 512)


def _na_bias_kernel(rpb_ref, o_ref, *, kh):
    cq = lax.broadcasted_iota(jnp.int32, (GRID_W, GRID_W), 0)
    ck = lax.broadcasted_iota(jnp.int32, (GRID_W, GRID_W), 1)
    cs = jnp.clip(cq - WIN_W // 2, 0, GRID_W - WIN_W)
    valid = jnp.logical_and(ck >= cs, ck < cs + WIN_W)
    tiles = []
    for dr in range(2 * WIN_H - 1):
        v = jnp.broadcast_to(rpb_ref[dr:dr + 1, :], (GRID_W, LANES)) * LOG2E
        t = pltpu.roll(v, LANES - (WIN_W - 1), axis=1, stride=1, stride_axis=0)[:, :GRID_W]
        tiles.append(jnp.where(valid, t, NEG_BIAS))
    for off in range(kh):
        o_ref[off] = jnp.concatenate([tiles[i - off + WIN_H - 1] for i in range(kh)], axis=1)


def _na_bias_table(rpb, kh):
    heads, nr, nc = rpb.shape
    assert nr <= 16 and nc <= LANES and GRID_W + WIN_W <= LANES
    padded = jnp.zeros((heads, 16, LANES), F32).at[:, :nr, :nc].set(rpb.astype(F32))
    return _call(
        functools.partial(_na_bias_kernel, kh=kh), grid=(heads,),
        in_specs=[pl.BlockSpec((None, 16, LANES), lambda hd: (hd, 0, 0))],
        out_specs=pl.BlockSpec((kh, None, GRID_W, kh * GRID_W), lambda hd: (0, hd, 0, 0)),
        out_shape=jax.ShapeDtypeStruct((kh, heads, GRID_W, kh * GRID_W), F32), name="nbr_bias",
    )(padded)


def _na_kernel(*refs, dh, kh, nrows, ctx, rq):
    bm_refs = refs[:rq]
    q_ref, k_ref, v_ref, o_ref = refs[rq:]
    nloc = kh * GRID_W
    chains = []
    for qi in range(rq):
        r = pl.program_id(2) * rq + qi
        rs = jnp.clip(r - kh // 2, 0, nrows - kh)
        start = pl.multiple_of(ctx + rs * GRID_W, GRID_W)
        qrows = slice(qi * GRID_W, (qi + 1) * GRID_W)
        for hd in range(q_ref.shape[1] // dh):
            cols = slice(hd * dh, (hd + 1) * dh)
            qh = q_ref[qrows, cols]
            sl = lax.dot_general(qh, k_ref[pl.ds(start, nloc), cols], _NT,
                                 preferred_element_type=F32) + bm_refs[qi][hd]
            sc = lax.dot_general(qh, k_ref[0:ctx, cols], _NT, preferred_element_type=F32)
            chains.append((qrows, cols, start, sl, sc))
    probs = []
    for _, _, _, sl, sc in chains:
        m = jnp.maximum(jnp.max(sl, axis=-1, keepdims=True), jnp.max(sc, axis=-1, keepdims=True))
        el = jnp.exp2(sl - m)
        ec = jnp.exp2(sc - m)
        den = jnp.sum(el, axis=-1, keepdims=True) + jnp.sum(ec, axis=-1, keepdims=True)
        probs.append((el.astype(v_ref.dtype), ec.astype(v_ref.dtype), 1.0 / den))
    for (qrows, cols, start, _, _), (el, ec, inv) in zip(chains, probs):
        o = (jnp.dot(el, v_ref[pl.ds(start, nloc), cols], preferred_element_type=F32)
             + jnp.dot(ec, v_ref[0:ctx, cols], preferred_element_type=F32))
        o_ref[qrows, cols] = (o * inv).astype(o_ref.dtype)


def _neighbourhood_attention(h, rows, w_qkv, w_o, rpb):
    t, d = h.shape
    heads = rpb.shape[0]
    dh = d // heads
    nrows = rows.lat // GRID_W
    kh = min(WIN_H, nrows)
    qkv = _mm(h, w_qkv, 3 * d, 512, q_cols=d, q_scale=dh ** -0.5 * LOG2E)
    bm = _na_bias_table(rpb, kh)
    hg = 4
    rq = 4
    hw = hg * dh
    s, c = rows.seg, rows.ctx
    assert nrows % rq == 0

    def bm_spec(qi):
        def index(b, g, r):
            row = r * rq + qi
            return (row - jnp.clip(row - kh // 2, 0, nrows - kh), g, 0, 0)
        return pl.BlockSpec((None, hg, GRID_W, kh * GRID_W), index)

    o = _call(
        functools.partial(_na_kernel, dh=dh, kh=kh, nrows=nrows, ctx=c, rq=rq),
        grid=(rows.batch, heads // hg, nrows // rq),
        in_specs=[bm_spec(qi) for qi in range(rq)] + [
            pl.BlockSpec((rq * GRID_W, hw), lambda b, g, r: ((b * s + c) // (rq * GRID_W) + r, g)),
            pl.BlockSpec((s, hw), lambda b, g, r: (b, d // hw + g)),
            pl.BlockSpec((s, hw), lambda b, g, r: (b, 2 * d // hw + g))],
        out_specs=pl.BlockSpec((rq * GRID_W, hw), lambda b, g, r: (b * (nrows // rq) + r, g)),
        out_shape=jax.ShapeDtypeStruct((rows.batch * rows.lat, d), BF16), name="nbr_attn",
    )(*([bm] * rq), qkv, qkv, qkv)
    return _mm(o, w_o, d, 512)


def kernel(x, c, ctx, c_ctx, ada_down, ada_up, ada_b, ln_g, ln_b, ffn_w_in, ffn_w_out,
           ssm_a_re, ssm_a_im, ssm_log_dt, ssm_b_re, ssm_b_im, ssm_c_re, ssm_c_im, ssm_d,
           ssm_w_glu, da_w_qkv, da_w_o, da_lambda, da_subln_g,
           cm_w_in, cm_ln_g, cm_ln_b, cm_w_s, cm_b_s, cm_w_out,
           na_w_qkv, na_w_o, na_rpb):
    bsz, seq, d = x.shape
    nctx = ctx.shape[1]
    depth = ada_down.shape[0]
    assert depth == 4 and bsz + 1 <= 8, "one layer per mixer kind; the last layer drops the context stream"
    assert nctx % ROW_TILE == 0 and seq % ROW_TILE == 0 and seq % GRID_W == 0
    alpha = (2 * depth) ** 0.25

    cond = jnp.zeros((8, d), F32).at[:bsz].set(c).at[bsz].set(c_ctx)
    mods = _adaln(cond, ada_down, ada_up, ada_b).reshape(depth, 8, 1, 6 * d)
    uni = _Rows(bsz, nctx, seq, True)
    lat = _Rows(bsz, nctx, seq, False)

    xs = jnp.concatenate([ctx, x], axis=1).reshape(uni.total, d)
    w_out_bf16 = ffn_w_out.astype(BF16)
    h = None
    for i in range(depth):
        rows = uni
        if i == 0:
            y = _s5_mixer(xs, mods[0], uni, ssm_a_re[0], ssm_a_im[0], ssm_log_dt[0], ssm_b_re[0], ssm_b_im[0],
                          ssm_c_re[0], ssm_c_im[0], ssm_d[0], ssm_w_glu[0])
        elif i == 1:
            y = _diff_attention(h, uni, da_w_qkv[0], da_w_o[0], da_lambda[0], da_subln_g[0],
                                0.8 - 0.6 * math.exp(-0.3 * i))
        elif i == 2:
            y = _chunk_mlp(h, cm_w_in[0], cm_ln_g[0], cm_ln_b[0], cm_w_s[0], cm_b_s[0], cm_w_out[0])
        else:
            y = _neighbourhood_attention(h, uni, na_w_qkv[0], na_w_o[0], na_rpb[0])
            rows = lat
        xs, h = _ln_res(xs, uni, y, rows, mods[i], 2, ln_g[i, 0], ln_b[i, 0], alpha,
                        next_mods=mods[i], next_chunks=(3, 4))
        y = _ffn(h, ffn_w_in, w_out_bf16, i)
        if i + 1 < depth:
            xs, h = _ln_res(xs, rows, y, rows, mods[i], 5, ln_g[i, 1], ln_b[i, 1], alpha,
                            next_mods=mods[i + 1], next_chunks=(0, 1))
        else:
            xs, _ = _ln_res(xs, rows, y, rows, mods[i], 5, ln_g[i, 1], ln_b[i, 1], alpha)
    return xs.reshape(bsz, seq, d)
```

```python
import functools
import math
from typing import NamedTuple

import jax
import jax.numpy as jnp
from jax import lax
from jax.experimental import pallas as pl
from jax.experimental.pallas import tpu as pltpu

F32 = jnp.float32
BF16 = jnp.bfloat16

GRID_W = 64
LN_EPS = 1e-5
SSM_CHUNK = 16
ROPE_THETA = 10000.0
CM_CHUNK = 128
WIN_H = 8
WIN_W = 16
NEG_BIAS = -1e30
LOG2E = 1.4426950408889634

LANES = 128
ROW_TILE = 256
EW_ROW_TILE = 256
VMEM_LIMIT_MB = 56

_NT = (((1,), (1,)), ((), ()))


class _Rows(NamedTuple):
    batch: int
    ctx: int
    lat: int
    with_ctx: bool

    @property
    def seg(self):
        return (self.ctx if self.with_ctx else 0) + self.lat

    @property
    def total(self):
        return self.batch * self.seg

    def mod_row(self, i, tm):
        per = self.seg // tm
        b = i // per
        if not self.with_ctx:
            return b
        return jnp.where(i % per < self.ctx // tm, self.batch, b)

    def tile_in(self, other, i, tm):
        if other.with_ctx == self.with_ctx:
            return i
        assert other.with_ctx and not self.with_ctx
        return i + (i // (self.lat // tm) + 1) * (self.ctx // tm)


def _call(kernel, *, grid, in_specs, out_specs, out_shape, scratch_shapes=(), aliases=None, name=None):
    return pl.pallas_call(
        kernel, grid=grid, in_specs=in_specs, out_specs=out_specs, out_shape=out_shape,
        scratch_shapes=list(scratch_shapes), input_output_aliases=aliases or {}, name=name,
        compiler_params=pltpu.CompilerParams(
            dimension_semantics=("arbitrary",) * len(grid),
            vmem_limit_bytes=VMEM_LIMIT_MB << 20))


def _gelu(x):
    return 0.5 * x * (1.0 + lax.erf(x * (2.0 ** -0.5)))


def _adaln_kernel(c_ref, wd_ref, wu_ref, b_ref, o_ref, t_scr):
    @pl.when(pl.program_id(1) == 0)
    def _():
        c = c_ref[...]
        t_scr[...] = jnp.dot(c * jax.nn.sigmoid(c), wd_ref[...], precision=lax.Precision.HIGHEST,
                             preferred_element_type=F32)

    o_ref[...] = jnp.dot(t_scr[...], wu_ref[...], precision=lax.Precision.HIGHEST,
                         preferred_element_type=F32) + b_ref[...]


def _adaln(cond, w_down, w_up, b):
    depth, d, r = w_down.shape
    n = w_up.shape[2]
    tn = d
    return _call(
        _adaln_kernel, grid=(depth, n // tn),
        in_specs=[pl.BlockSpec((8, d), lambda l, j: (0, 0)),
                  pl.BlockSpec((None, d, r), lambda l, j: (l, 0, 0)),
                  pl.BlockSpec((None, r, tn), lambda l, j: (l, 0, j)),
                  pl.BlockSpec((None, 1, tn), lambda l, j: (l, 0, j))],
        out_specs=pl.BlockSpec((None, 8, tn), lambda l, j: (l, 0, j)),
        out_shape=jax.ShapeDtypeStruct((depth, 8, n), F32),
        scratch_shapes=[pltpu.VMEM((8, r), F32)], name="adaln",
    )(cond, w_down, w_up, b.reshape(depth, 1, n))


def _mod_spec(rows, tm, d, chunk):
    return pl.BlockSpec((None, 1, d), lambda i: (rows.mod_row(i, tm), 0, chunk))


def _ln_res_kernel(*refs, alpha, eps, has_next):
    if has_next:
        x_ref, y_ref, gate_ref, g_ref, b_ref, sh_ref, sc_ref, xo_ref, ho_ref = refs
    else:
        x_ref, y_ref, gate_ref, g_ref, b_ref, xo_ref = refs
    z = alpha * x_ref[...] + gate_ref[...] * y_ref[...].astype(F32)
    mu = jnp.mean(z, axis=-1, keepdims=True)
    dz = z - mu
    var = jnp.mean(dz * dz, axis=-1, keepdims=True)
    xn = dz * lax.rsqrt(var + eps) * g_ref[...] + b_ref[...]
    xo_ref[...] = xn
    if has_next:
        ho_ref[...] = (xn * (1.0 + sc_ref[...]) + sh_ref[...]).astype(ho_ref.dtype)


def _ln_res(x, x_rows, y, rows, gate_mods, gate_chunk, ln_g, ln_b, alpha, next_mods=None, next_chunks=None):
    t, d = y.shape
    tm = EW_ROW_TILE
    has_next = next_mods is not None
    row_spec = pl.BlockSpec((tm, d), lambda i: (i, 0))
    vec_spec = pl.BlockSpec((1, d), lambda i: (0, 0))
    in_specs = [pl.BlockSpec((tm, d), lambda i: (rows.tile_in(x_rows, i, tm), 0)), row_spec,
                _mod_spec(rows, tm, d, gate_chunk), vec_spec, vec_spec]
    args = [x, y, gate_mods, ln_g.reshape(1, d), ln_b.reshape(1, d)]
    out_specs = [row_spec]
    out_shape = [jax.ShapeDtypeStruct((t, d), F32)]
    if has_next:
        in_specs += [_mod_spec(rows, tm, d, next_chunks[0]), _mod_spec(rows, tm, d, next_chunks[1])]
        args += [next_mods, next_mods]
        out_specs.append(row_spec)
        out_shape.append(jax.ShapeDtypeStruct((t, d), BF16))
    out = _call(
        functools.partial(_ln_res_kernel, alpha=alpha, eps=LN_EPS, has_next=has_next),
        grid=(t // tm,), in_specs=in_specs, out_specs=out_specs, out_shape=out_shape, name="ln_res",
    )(*args)
    return (out[0], out[1]) if has_next else (out[0], None)


def _swap32(a):
    lane = lax.broadcasted_iota(jnp.int32, a.shape, 1)
    return jnp.where(lane % 64 < 32, pltpu.roll(a, LANES - 32, axis=1), pltpu.roll(a, 32, axis=1))


def _mm_kernel(*refs, mode, nw, cast, q_tiles, q_scale):
    w_refs = refs[1:1 + nw]
    o_ref = refs[-1 - (nw if cast else 0)]
    if cast:
        scr = refs[len(refs) - nw:]

        @pl.when(pl.program_id(1) == 0)
        def _():
            for w_ref, s_ref in zip(w_refs, scr):
                s_ref[...] = w_ref[...].astype(s_ref.dtype)

        w_refs = scr
    x = refs[0][...]
    a = jnp.dot(x, w_refs[0][...], preferred_element_type=F32)
    if mode == "plain":
        out = a
    elif mode == "gelu":
        out = _gelu(a)
    elif mode == "swiglu":
        b = jnp.dot(x, w_refs[1][...], preferred_element_type=F32)
        out = a * jax.nn.sigmoid(a) * b
    elif mode == "glu":
        b = jnp.dot(x, w_refs[1][...], preferred_element_type=F32)
        out = a * jax.nn.sigmoid(b)
    elif mode == "rope":
        cos = refs[1 + nw][...]
        sin = refs[2 + nw][...]
        pieces = []
        for g in range(a.shape[1] // LANES):
            ag = a[:, g * LANES:(g + 1) * LANES]
            pieces.append(ag * cos + _swap32(ag) * sin)
        out = jnp.concatenate(pieces, axis=1)
    else:
        raise ValueError(mode)
    if q_tiles:
        out = out * jnp.where(pl.program_id(0) < q_tiles, q_scale, 1.0)
    o_ref[...] = out.astype(o_ref.dtype)


def _row_tile(t, k):
    budget = 24 << 20
    for nt in range(1, t // 16 + 1):
        tm = t // nt
        if t % nt == 0 and tm % 16 == 0 and 4 * tm * k <= budget:
            return tm
    raise ValueError((t, k))


def _mm(x, w, n_out, tn, *, mode="plain", layer=None, col0=0, second_col=None, rope=None, q_cols=0,
        q_scale=1.0, out_dtype=BF16):
    t, k = x.shape
    tm = _row_tile(t, k)
    assert n_out % tn == 0 and col0 % tn == 0 and q_cols % tn == 0
    cast = w.dtype != BF16
    offs = [col0 // tn] + ([second_col // tn] if mode in ("swiglu", "glu") else [])
    in_specs = [pl.BlockSpec((tm, k), lambda j, i: (i, 0))]
    if layer is None:
        in_specs += [pl.BlockSpec((k, tn), lambda j, i, off=off: (0, j + off)) for off in offs]
    else:
        in_specs += [pl.BlockSpec((None, k, tn), lambda j, i, off=off: (layer, 0, j + off)) for off in offs]
    args = [x] + [w] * len(offs)
    if mode == "rope":
        in_specs += [pl.BlockSpec((tm, LANES), lambda j, i: (i, 0))] * 2
        args += list(rope)
    return _call(
        functools.partial(_mm_kernel, mode=mode, nw=len(offs), cast=cast, q_tiles=q_cols // tn, q_scale=q_scale),
        grid=(n_out // tn, t // tm),
        in_specs=in_specs, out_specs=pl.BlockSpec((tm, tn), lambda j, i: (i, j)),
        out_shape=jax.ShapeDtypeStruct((t, n_out), out_dtype),
        scratch_shapes=[pltpu.VMEM((k, tn), BF16)] * len(offs) if cast else (), name="mm_" + mode,
    )(*args)


def _ffn(h, w_in, w_out, layer):
    f = w_out.shape[1]
    a = _mm(h, w_in, f, 256, mode="swiglu", layer=layer, second_col=f)
    return _mm(a, w_out, w_out.shape[2], 512, layer=layer)


def _cmul(ar, ai, br, bi):
    return ar * br - ai * bi, ar * bi + ai * br


def _s5_param_kernel(are_ref, aim_ref, ldt_ref, btr_ref, bti_ref, cre_ref, cim_ref,
                     kc_ref, bz_ref, coef_ref, *, q):
    for g in range(cre_ref.shape[0]):
        _s5_param_group(are_ref.at[:, g], aim_ref.at[:, g], ldt_ref.at[:, g], btr_ref.at[g], bti_ref.at[g],
                        cre_ref.at[g], cim_ref.at[g], kc_ref.at[g], bz_ref.at[g], coef_ref.at[g], q=q)


def _s5_param_group(are_ref, aim_ref, ldt_ref, btr_ref, bti_ref, cre_ref, cim_ref,
                    kc_ref, bz_ref, coef_ref, *, q):
    cre = cre_ref[...]
    cim = cim_ref[...]
    gc, p = cre.shape
    w = q * gc
    lane = lax.broadcasted_iota(jnp.int32, (gc, w), 1)
    kmat = None
    cmt = []
    bz = []
    coef_rows = []
    for d in range(2):
        lr = are_ref[d]
        li = aim_ref[d]
        dt = jnp.exp(ldt_ref[d])
        ea = jnp.exp(lr * dt)
        ar = ea * jnp.cos(li * dt)
        ai = ea * jnp.sin(li * dt)
        den = lr * lr + li * li
        nr = ar - 1.0
        kr = (nr * lr + ai * li) / den
        ki = (ai * lr - nr * li) / den
        bbr, bbi = _cmul(kr, ki, btr_ref[...], bti_ref[...])
        pr = [jnp.ones_like(ar)]
        pi = [jnp.zeros_like(ar)]
        for _ in range(q):
            nr_, ni_ = _cmul(pr[-1], pi[-1], ar, ai)
            pr.append(nr_)
            pi.append(ni_)
        cp = [_cmul(cre, cim, pr[k], pi[k]) for k in range(q + 1)]
        bp = [_cmul(bbr, bbi, pr[k], pi[k]) for k in range(q)]
        bb = jnp.concatenate([bbr, -bbi], axis=1)
        if d == 0:
            vall = jnp.concatenate([jnp.concatenate(cp[l], axis=1) for l in range(q)], axis=0)
        else:
            vall = jnp.concatenate([jnp.concatenate(cp[q - 1 - l], axis=1) for l in range(q)], axis=0)
        mt = lax.dot_general(bb, vall, _NT, precision=lax.Precision.HIGHEST,
                             preferred_element_type=F32)
        blocks = []
        for s in range(q):
            if d == 0:
                n = gc * s
                blk = mt if n == 0 else jnp.where(lane >= n, pltpu.roll(mt, n, axis=1), 0.0)
            else:
                n = gc * (q - 1 - s)
                blk = mt if n == 0 else jnp.where(lane < w - n, pltpu.roll(mt, w - n, axis=1), 0.0)
            blocks.append(blk)
        kd = jnp.concatenate(blocks, axis=0)
        kmat = kd if kmat is None else kmat + kd
        zs = [bp[q - 1 - s] if d == 0 else bp[s] for s in range(q)]
        bz.append(jnp.concatenate([jnp.concatenate([zr, zi], axis=1) for zr, zi in zs], axis=0))
        bz.append(jnp.concatenate([jnp.concatenate([zi, zr], axis=1) for zr, zi in zs], axis=0))
        cs = [cp[t + 1] if d == 0 else cp[q - t] for t in range(q)]
        cmt.append(jnp.concatenate([jnp.concatenate([vr, -vi], axis=1) for vr, vi in cs], axis=0))
        coef_rows.append(jnp.concatenate([pr[q], pr[q]], axis=1))
        coef_rows.append(jnp.concatenate([-pi[q], pi[q]], axis=1))
    kc_ref[0:w, :] = kmat.astype(kc_ref.dtype)
    cm = jnp.concatenate(cmt, axis=1).T
    kc_ref[w:w + 4 * p, :] = cm.astype(kc_ref.dtype)
    bz_ref[...] = jnp.concatenate(bz, axis=1).astype(bz_ref.dtype)
    coef_rows += [jnp.zeros_like(coef_rows[0])] * 4
    coef_ref[...] = jnp.concatenate(coef_rows, axis=0)


def _s5_params(a_re, a_im, log_dt, b_re, b_im, c_re, c_im):
    _, g, p = a_re.shape
    gc = c_re.shape[1]
    q = SSM_CHUNK
    w = q * gc
    assert 2 * p == LANES
    ng = 4
    assert g % ng == 0
    dir_spec = pl.BlockSpec((2, ng, 1, p), lambda i: (0, i, 0, 0))
    mat_spec = pl.BlockSpec((ng, gc, p), lambda i: (i, 0, 0))
    ldt = jnp.broadcast_to(log_dt[:, :, None, None], (2, g, 1, p))
    return _call(
        functools.partial(_s5_param_kernel, q=q), grid=(g // ng,),
        in_specs=[dir_spec, dir_spec, dir_spec, mat_spec, mat_spec, mat_spec, mat_spec],
        out_specs=[pl.BlockSpec((ng, w + 4 * p, w), lambda i: (i, 0, 0)),
                   pl.BlockSpec((ng, w, 8 * p), lambda i: (i, 0, 0)),
                   pl.BlockSpec((ng, 8, 2 * p), lambda i: (i, 0, 0))],
        out_shape=[jax.ShapeDtypeStruct((g, w + 4 * p, w), BF16),
                   jax.ShapeDtypeStruct((g, w, 8 * p), BF16),
                   jax.ShapeDtypeStruct((g, 8, 2 * p), F32)], name="s5_params",
    )(a_re.reshape(2, g, 1, p), a_im.reshape(2, g, 1, p), ldt,
      jnp.swapaxes(b_re, 1, 2), jnp.swapaxes(b_im, 1, 2), c_re, c_im)


def _s5_scan_kernel(u_ref, kc_ref, bz_ref, coef_ref, y_ref, *scr, nb, nsub, nsub_ctx):
    ng, nrow, w = u_ref.shape
    z_scr, s_scr = scr[:4], scr[4:]
    for g in range(ng):
        z = jnp.dot(u_ref[g], bz_ref[g], preferred_element_type=F32)
        for k in range(4):
            z_scr[k][pl.ds(g, nrow, stride=ng), :] = z[:, k * LANES:(k + 1) * LANES]
    coef = [jnp.concatenate([coef_ref[g, k:k + 1, :] for g in range(ng)], axis=0) for k in range(4)]

    def body(n, carry):
        jb = jnp.where(n < nsub_ctx, nsub_ctx - 1 - n, nsub + nsub_ctx - 1 - n)
        new = []
        k = 0
        for b in range(nb):
            for d, j in ((0, n), (1, jb)):
                h, hs = carry[k], carry[k + 1]
                k += 2
                base = pl.multiple_of((b * nsub + j) * ng, ng)
                s_scr[d][pl.ds(base, ng), :] = h
                p1, p2 = coef[2 * d], coef[2 * d + 1]
                new.append(p1 * h + p2 * hs + z_scr[2 * d][pl.ds(base, ng), :])
                new.append(p1 * hs - p2 * h + z_scr[2 * d + 1][pl.ds(base, ng), :])
        return tuple(new)

    zero = jnp.zeros((ng, LANES), F32)
    lax.fori_loop(0, nsub, body, (zero,) * (4 * nb), unroll=4)
    for g in range(ng):
        u = u_ref[g]
        st = jnp.concatenate([s_scr[d][pl.ds(g, nrow, stride=ng), :] for d in range(2)], axis=1)
        y = jnp.dot(u, kc_ref[g, 0:w, :], preferred_element_type=F32)
        y += jnp.dot(st.astype(u.dtype), kc_ref[g, w:w + 2 * LANES, :], preferred_element_type=F32)
        y_ref[g] = y.astype(y_ref.dtype)


def _group_lane_masks(shape, gc):
    lane_group = lax.broadcasted_iota(jnp.int32, shape, 1) // gc
    return [lane_group == g for g in range(LANES // gc)]


def _s5_pack_kernel(x_ref, sh_ref, sc_ref, h_ref, u_ref, h_scr, *, q, gc):
    h = x_ref[...] * (1.0 + sc_ref[...]) + sh_ref[...]
    h_ref[...] = h.astype(h_ref.dtype)
    tm, d = h.shape
    nr = tm // q
    gpb = LANES // gc
    masks = _group_lane_masks((nr, LANES), gc)
    for blk in range(d // LANES):
        h_scr[blk] = h[:, blk * LANES:(blk + 1) * LANES]
        acc = [[None] * (q // gpb) for _ in range(gpb)]
        for s in range(q):
            xs = h_scr[blk, pl.ds(s, nr, stride=q), :]
            sb, col = s % gpb, s // gpb
            for dl in range(gpb):
                g = (sb - dl) % gpb
                moved = xs if dl == 0 else pltpu.roll(xs, dl * gc, axis=1)
                prev = acc[g][col]
                acc[g][col] = jnp.where(masks[sb], moved, 0.0 if prev is None else prev)
        for g in range(gpb):
            u_ref[blk * gpb + g] = jnp.concatenate(acc[g], axis=1).astype(u_ref.dtype)


def _s5_finish_kernel(y_ref, h_ref, d_ref, o_ref, y_scr, *, q, gc):
    nr = y_ref.shape[1]
    d = h_ref.shape[1]
    gpb = LANES // gc
    masks = _group_lane_masks((nr, LANES), gc)
    for blk in range(d // LANES):
        lanes = slice(blk * LANES, (blk + 1) * LANES)
        ys = [y_ref[blk * gpb + g].astype(F32) for g in range(gpb)]
        for s in range(q):
            sb, col = s % gpb, s // gpb
            xs = None
            for dl in range(gpb):
                g = (sb + dl) % gpb
                piece = ys[g][:, col * LANES:(col + 1) * LANES]
                moved = piece if dl == 0 else pltpu.roll(piece, dl * gc, axis=1)
                xs = jnp.where(masks[g], moved, 0.0 if xs is None else xs)
            y_scr[blk, pl.ds(s, nr, stride=q), :] = xs
        v = (y_scr[blk] + d_ref[:, lanes] * h_ref[:, lanes].astype(F32)).astype(h_ref.dtype)
        o_ref[:, lanes] = _gelu(v.astype(F32)).astype(o_ref.dtype)


def _s5_mixer(x, mods, rows, a_re, a_im, log_dt, b_re, b_im, c_re, c_im, d_skip, w_glu):
    t, d = x.shape
    g = a_re.shape[1]
    gc = d // g
    q = SSM_CHUNK
    w = q * gc
    r = t // q
    assert LANES % gc == 0 and w % LANES == 0 and q % (LANES // gc) == 0
    kc, bz, coef = _s5_params(a_re, a_im, log_dt, b_re, b_im, c_re, c_im)
    tm = ROW_TILE
    nr = tm // q
    gps = 8
    assert g % gps == 0
    row_spec = pl.BlockSpec((tm, d), lambda i: (i, 0))
    grp_spec = pl.BlockSpec((g, nr, w), lambda i: (0, i, 0))
    h, u = _call(
        functools.partial(_s5_pack_kernel, q=q, gc=gc), grid=(t // tm,),
        in_specs=[row_spec, _mod_spec(rows, tm, d, 0), _mod_spec(rows, tm, d, 1)],
        out_specs=[row_spec, grp_spec],
        out_shape=[jax.ShapeDtypeStruct((t, d), BF16), jax.ShapeDtypeStruct((g, r, w), BF16)],
        scratch_shapes=[pltpu.VMEM((d // LANES, tm, LANES), F32)], name="s5_pack",
    )(x, mods, mods)
    y = _call(
        functools.partial(_s5_scan_kernel, nb=rows.batch, nsub=rows.seg // q, nsub_ctx=rows.ctx // q),
        grid=(g // gps,),
        in_specs=[pl.BlockSpec((gps, r, w), lambda i: (i, 0, 0)),
                  pl.BlockSpec((gps,) + kc.shape[1:], lambda i: (i, 0, 0)),
                  pl.BlockSpec((gps,) + bz.shape[1:], lambda i: (i, 0, 0)),
                  pl.BlockSpec((gps,) + coef.shape[1:], lambda i: (i, 0, 0))],
        out_specs=pl.BlockSpec((gps, r, w), lambda i: (i, 0, 0)),
        out_shape=jax.ShapeDtypeStruct((g, r, w), BF16),
        scratch_shapes=[pltpu.VMEM((r * gps, LANES), F32)] * 6,
        name="s5_scan",
    )(u, kc, bz, coef)
    v = _call(
        functools.partial(_s5_finish_kernel, q=q, gc=gc), grid=(t // tm,),
        in_specs=[grp_spec, row_spec, pl.BlockSpec((1, d), lambda i: (0, 0))],
        out_specs=row_spec, out_shape=jax.ShapeDtypeStruct((t, d), BF16),
        scratch_shapes=[pltpu.VMEM((d // LANES, tm, LANES), F32)], name="s5_finish",
    )(y, h, d_skip.reshape(1, d))
    return _mm(v, w_glu, d, 256, mode="glu", second_col=d)


def _rope_tables(rows):
    half = LANES // 2
    quarter = half // 2
    tpos = jnp.arange(rows.lat)
    pos = jnp.stack([tpos // GRID_W, tpos % GRID_W], axis=1).astype(F32)
    inv = ROPE_THETA ** (-jnp.arange(quarter, dtype=F32) * 2.0 / half)
    ang = pos[:, :, None] * inv[None, None, :]
    ang = jnp.repeat(ang[:, :, None, :], 2, axis=2).reshape(rows.lat, LANES)
    sign = jnp.tile(jnp.repeat(jnp.array([-1.0, 1.0], F32), quarter), 2)
    cos = jnp.cos(ang)
    sin = jnp.sin(ang) * sign
    cos = jnp.concatenate([jnp.ones((rows.ctx, LANES), F32), cos], axis=0)
    sin = jnp.concatenate([jnp.zeros((rows.ctx, LANES), F32), sin], axis=0)
    return jnp.tile(cos, (rows.batch, 1)), jnp.tile(sin, (rows.batch, 1))


def _da_kernel(lam_ref, g_ref, q_ref, k_ref, v_ref, o_ref, s_scr, e_scr, *, lam_init, eps, tk):
    lp = lam_ref[...]
    dh = lp.shape[1]
    lam = (jnp.exp(jnp.sum(lp[0:1] * lp[1:2], axis=-1, keepdims=True))
           - jnp.exp(jnp.sum(lp[2:3] * lp[3:4], axis=-1, keepdims=True)) + lam_init)
    tq = q_ref.shape[0]
    nchunk = k_ref.shape[0] // tk
    nunit = q_ref.shape[1] // dh
    row_max = []
    for u in range(nunit):
        qu = q_ref[:, u * dh:(u + 1) * dh]
        m = jnp.full((tq, LANES), -jnp.inf, F32)
        for c in range(nchunk):
            s = lax.dot_general(qu, k_ref[c * tk:(c + 1) * tk, u * dh:(u + 1) * dh], _NT,
                                preferred_element_type=F32)
            s_scr[u, :, c * tk:(c + 1) * tk] = s
            for j in range(tk // LANES):
                m = jnp.maximum(m, s[:, j * LANES:(j + 1) * LANES])
        row_max.append(jnp.max(m, axis=-1, keepdims=True))
    outs = []
    for u in range(nunit):
        mrow = row_max[u]
        den = jnp.zeros((tq, LANES), F32)
        for c in range(nchunk):
            e = jnp.exp2(s_scr[u, :, c * tk:(c + 1) * tk] - mrow)
            for j in range(tk // LANES):
                den = den + e[:, j * LANES:(j + 1) * LANES]
            e_scr[u, :, c * tk:(c + 1) * tk] = e.astype(e_scr.dtype)
        den = jnp.sum(den, axis=-1, keepdims=True)
        hd = u // 2
        outs.append(jnp.dot(e_scr[u], v_ref[:, hd * 2 * dh:(hd + 1) * 2 * dh],
                            preferred_element_type=F32) * (1.0 / den))
    for hd in range(nunit // 2):
        o = outs[2 * hd] - lam * outs[2 * hd + 1]
        on = o * lax.rsqrt(jnp.mean(o * o, axis=-1, keepdims=True) + eps)
        o_ref[:, hd * 2 * dh:(hd + 1) * 2 * dh] = (on * g_ref[...] * (1.0 - lam_init)).astype(o_ref.dtype)


def _diff_attention(h, rows, w_qkv, w_o, lam_p, subln_g, lam_init):
    t, d = h.shape
    dh = lam_p.shape[1]
    hb = 2
    hw = hb * 2 * dh
    heads = d // hw
    cos, sin = _rope_tables(rows)
    qk = _mm(h, w_qkv, 2 * d, 512, mode="rope", rope=(cos, sin), q_cols=d, q_scale=dh ** -0.5 * LOG2E)
    v = _mm(h, w_qkv, d, 512, col0=2 * d)
    tk = ROW_TILE
    body = functools.partial(_da_kernel, lam_init=lam_init, eps=LN_EPS, tk=tk)

    def scratch(nq, nk):
        return [pltpu.VMEM((2 * hb, nq, nk), F32), pltpu.VMEM((2 * hb, nq, nk), BF16)]

    lam_spec = pl.BlockSpec(lam_p.shape, lambda b, hd, i: (0, 0))
    g_spec = pl.BlockSpec((1, 2 * dh), lambda b, hd, i: (0, 0))
    tq = ROW_TILE
    s, c = rows.seg, rows.ctx
    o = _call(
        body, grid=(rows.batch, heads, rows.lat // tq),
        in_specs=[lam_spec, g_spec,
                  pl.BlockSpec((tq, hw), lambda b, hd, i: ((b * s + c) // tq + i, hd)),
                  pl.BlockSpec((s, hw), lambda b, hd, i: (b, heads + hd)),
                  pl.BlockSpec((s, hw), lambda b, hd, i: (b, hd))],
        out_specs=pl.BlockSpec((tq, hw), lambda b, hd, i: ((b * s + c) // tq + i, hd)),
        out_shape=jax.ShapeDtypeStruct((t, d), BF16), scratch_shapes=scratch(tq, s), name="diff_attn_lat",
    )(lam_p, subln_g.reshape(1, 2 * dh), qk, qk, v)
    o = _call(
        lambda lam_ref, g_ref, q_ref, k_ref, v_ref, prev_ref, o_ref, s_scr, e_scr: body(
            lam_ref, g_ref, q_ref, k_ref, v_ref, o_ref, s_scr, e_scr),
        grid=(rows.batch, heads, 1),
        in_specs=[lam_spec, g_spec,
                  pl.BlockSpec((c, hw), lambda b, hd, i: (b * s // c, hd)),
                  pl.BlockSpec((c, hw), lambda b, hd, i: (b * s // c, heads + hd)),
                  pl.BlockSpec((c, hw), lambda b, hd, i: (b * s // c, hd)),
                  pl.BlockSpec(memory_space=pl.ANY)],
        out_specs=pl.BlockSpec((c, hw), lambda b, hd, i: (b * s // c, hd)),
        out_shape=jax.ShapeDtypeStruct((t, d), BF16), aliases={5: 0}, scratch_shapes=scratch(c, c),
        name="diff_attn_ctx",
    )(lam_p, subln_g.reshape(1, 2 * dh), qk, qk, v, o)
    return _mm(o, w_o, d, 512)


def _cm_kernel(u_ref, v_ref, g_ref, b_ref, ws_ref, bs_ref, o_ref, *, eps):
    v = v_ref[...].astype(F32)
    mu = jnp.mean(v, axis=-1, keepdims=True)
    dv = v - mu
    var = jnp.mean(dv * dv, axis=-1, keepdims=True)
    vn = (dv * lax.rsqrt(var + eps) * g_ref[...] + b_ref[...]).astype(ws_ref.dtype)
    groups = ws_ref.shape[0]
    gw = vn.shape[1] // groups
    for g in range(groups):
        cols = slice(g * gw, (g + 1) * gw)
        vm = jnp.dot(ws_ref[g], vn[:, cols], preferred_element_type=F32) + bs_ref[:, g:g + 1]
        o_ref[:, cols] = (u_ref[:, cols].astype(F32) * vm).astype(o_ref.dtype)


def _chunk_mlp(h, w_in, ln_g, ln_b, w_s, b_s, w_out):
    t, d = h.shape
    uv = _mm(h, w_in, 2 * d, 512, mode="gelu")
    tc = CM_CHUNK
    groups = w_s.shape[0]
    vec_spec = pl.BlockSpec((1, d), lambda i: (0, 0))
    gated = _call(
        functools.partial(_cm_kernel, eps=LN_EPS), grid=(t // tc,),
        in_specs=[pl.BlockSpec((tc, d), lambda i: (i, 0)), pl.BlockSpec((tc, d), lambda i: (i, 1)),
                  vec_spec, vec_spec,
                  pl.BlockSpec((groups, tc, tc), lambda i: (0, 0, 0)),
                  pl.BlockSpec((tc, groups), lambda i: (0, 0))],
        out_specs=pl.BlockSpec((tc, d), lambda i: (i, 0)),
        out_shape=jax.ShapeDtypeStruct((t, d), BF16), name="chunk_gate",
    )(uv, uv, ln_g.reshape(1, d), ln_b.reshape(1, d), w_s.astype(BF16), b_s.T)
    return _mm(gated, w_out, d, 512)


def _na_bias_kernel(rpb_ref, o_ref, *, kh):
    cq = lax.broadcasted_iota(jnp.int32, (GRID_W, GRID_W), 0)
    ck = lax.broadcasted_iota(jnp.int32, (GRID_W, GRID_W), 1)
    cs = jnp.clip(cq - WIN_W // 2, 0, GRID_W - WIN_W)
    valid = jnp.logical_and(ck >= cs, ck < cs + WIN_W)
    tiles = []
    for dr in range(2 * WIN_H - 1):
        v = jnp.broadcast_to(rpb_ref[dr:dr + 1, :], (GRID_W, LANES)) * LOG2E
        t = pltpu.roll(v, LANES - (WIN_W - 1), axis=1, stride=1, stride_axis=0)[:, :GRID_W]
        tiles.append(jnp.where(valid, t, NEG_BIAS))
    for off in range(kh):
        o_ref[off] = jnp.concatenate([tiles[i - off + WIN_H - 1] for i in range(kh)], axis=1)


def _na_bias_table(rpb, kh):
    heads, nr, nc = rpb.shape
    assert nr <= 16 and nc <= LANES and GRID_W + WIN_W <= LANES
    padded = jnp.zeros((heads, 16, LANES), F32).at[:, :nr, :nc].set(rpb.astype(F32))
    return _call(
        functools.partial(_na_bias_kernel, kh=kh), grid=(heads,),
        in_specs=[pl.BlockSpec((None, 16, LANES), lambda hd: (hd, 0, 0))],
        out_specs=pl.BlockSpec((kh, None, GRID_W, kh * GRID_W), lambda hd: (0, hd, 0, 0)),
        out_shape=jax.ShapeDtypeStruct((kh, heads, GRID_W, kh * GRID_W), F32), name="nbr_bias",
    )(padded)


def _na_kernel(*refs, dh, kh, nrows, ctx, rq):
    bm_refs = refs[:rq]
    q_ref, k_ref, v_ref, o_ref = refs[rq:]
    nloc = kh * GRID_W
    chains = []
    for qi in range(rq):
        r = pl.program_id(2) * rq + qi
        rs = jnp.clip(r - kh // 2, 0, nrows - kh)
        start = pl.multiple_of(ctx + rs * GRID_W, GRID_W)
        qrows = slice(qi * GRID_W, (qi + 1) * GRID_W)
        for hd in range(q_ref.shape[1] // dh):
            cols = slice(hd * dh, (hd + 1) * dh)
            qh = q_ref[qrows, cols]
            sl = lax.dot_general(qh, k_ref[pl.ds(start, nloc), cols], _NT,
                                 preferred_element_type=F32) + bm_refs[qi][hd]
            sc = lax.dot_general(qh, k_ref[0:ctx, cols], _NT, preferred_element_type=F32)
            chains.append((qrows, cols, start, sl, sc))
    probs = []
    for _, _, _, sl, sc in chains:
        m = jnp.maximum(jnp.max(sl, axis=-1, keepdims=True), jnp.max(sc, axis=-1, keepdims=True))
        el = jnp.exp2(sl - m)
        ec = jnp.exp2(sc - m)
        den = jnp.sum(el, axis=-1, keepdims=True) + jnp.sum(ec, axis=-1, keepdims=True)
        probs.append((el.astype(v_ref.dtype), ec.astype(v_ref.dtype), 1.0 / den))
    for (qrows, cols, start, _, _), (el, ec, inv) in zip(chains, probs):
        o = (jnp.dot(el, v_ref[pl.ds(start, nloc), cols], preferred_element_type=F32)
             + jnp.dot(ec, v_ref[0:ctx, cols], preferred_element_type=F32))
        o_ref[qrows, cols] = (o * inv).astype(o_ref.dtype)


def _neighbourhood_attention(h, rows, w_qkv, w_o, rpb):
    t, d = h.shape
    heads = rpb.shape[0]
    dh = d // heads
    nrows = rows.lat // GRID_W
    kh = min(WIN_H, nrows)
    qkv = _mm(h, w_qkv, 3 * d, 512, q_cols=d, q_scale=dh ** -0.5 * LOG2E)
    bm = _na_bias_table(rpb, kh)
    hg = 4
    rq = 4
    hw = hg * dh
    s, c = rows.seg, rows.ctx
    assert nrows % rq == 0 and s % (rq * GRID_W) == 0 and c % (rq * GRID_W) == 0

    def bm_spec(qi):
        def index(b, g, r):
            row = r * rq + qi
            return (row - jnp.clip(row - kh // 2, 0, nrows - kh), g, 0, 0)
        return pl.BlockSpec((None, hg, GRID_W, kh * GRID_W), index)

    o = _call(
        functools.partial(_na_kernel, dh=dh, kh=kh, nrows=nrows, ctx=c, rq=rq),
        grid=(rows.batch, heads // hg, nrows // rq),
        in_specs=[bm_spec(qi) for qi in range(rq)] + [
            pl.BlockSpec((rq * GRID_W, hw), lambda b, g, r: ((b * s + c) // (rq * GRID_W) + r, g)),
            pl.BlockSpec((s, hw), lambda b, g, r: (b, d // hw + g)),
            pl.BlockSpec((s, hw), lambda b, g, r: (b, 2 * d // hw + g))],
        out_specs=pl.BlockSpec((rq * GRID_W, hw), lambda b, g, r: (b * (nrows // rq) + r, g)),
        out_shape=jax.ShapeDtypeStruct((rows.batch * rows.lat, d), BF16), name="nbr_attn",
    )(*([bm] * rq), qkv, qkv, qkv)
    return _mm(o, w_o, d, 512)


def kernel(x, c, ctx, c_ctx, ada_down, ada_up, ada_b, ln_g, ln_b, ffn_w_in, ffn_w_out,
           ssm_a_re, ssm_a_im, ssm_log_dt, ssm_b_re, ssm_b_im, ssm_c_re, ssm_c_im, ssm_d,
           ssm_w_glu, da_w_qkv, da_w_o, da_lambda, da_subln_g,
           cm_w_in, cm_ln_g, cm_ln_b, cm_w_s, cm_b_s, cm_w_out,
           na_w_qkv, na_w_o, na_rpb):
    bsz, seq, d = x.shape
    nctx = ctx.shape[1]
    depth = ada_down.shape[0]
    assert depth == 4 and bsz + 1 <= 8, "one layer per mixer kind; the last layer drops the context stream"
    assert nctx % ROW_TILE == 0 and seq % ROW_TILE == 0 and seq % GRID_W == 0
    assert nctx % EW_ROW_TILE == 0 and seq % EW_ROW_TILE == 0
    alpha = (2 * depth) ** 0.25

    cond = jnp.zeros((8, d), F32).at[:bsz].set(c).at[bsz].set(c_ctx)
    mods = _adaln(cond, ada_down, ada_up, ada_b).reshape(depth, 8, 1, 6 * d)
    uni = _Rows(bsz, nctx, seq, True)
    lat = _Rows(bsz, nctx, seq, False)

    xs = jnp.concatenate([ctx, x], axis=1).reshape(uni.total, d)
    w_out_bf16 = ffn_w_out.astype(BF16)
    h = None
    for i in range(depth):
        rows = uni
        if i == 0:
            y = _s5_mixer(xs, mods[0], uni, ssm_a_re[0], ssm_a_im[0], ssm_log_dt[0], ssm_b_re[0], ssm_b_im[0],
                          ssm_c_re[0], ssm_c_im[0], ssm_d[0], ssm_w_glu[0])
        elif i == 1:
            y = _diff_attention(h, uni, da_w_qkv[0], da_w_o[0], da_lambda[0], da_subln_g[0],
                                0.8 - 0.6 * math.exp(-0.3 * i))
        elif i == 2:
            y = _chunk_mlp(h, cm_w_in[0], cm_ln_g[0], cm_ln_b[0], cm_w_s[0], cm_b_s[0], cm_w_out[0])
        else:
            y = _neighbourhood_attention(h, uni, na_w_qkv[0], na_w_o[0], na_rpb[0])
            rows = lat
        xs, h = _ln_res(xs, uni, y, rows, mods[i], 2, ln_g[i, 0], ln_b[i, 0], alpha,
                        next_mods=mods[i], next_chunks=(3, 4))
        y = _ffn(h, ffn_w_in, w_out_bf16, i)
        if i + 1 < depth:
            xs, h = _ln_res(xs, rows, y, rows, mods[i], 5, ln_g[i, 1], ln_b[i, 1], alpha,
                            next_mods=mods[i + 1], next_chunks=(0, 1))
        else:
            xs, _ = _ln_res(xs, rows, y, rows, mods[i], 5, ln_g[i, 1], ln_b[i, 1], alpha)
    return xs.reshape(bsz, seq, d)
```

```python
import functools
import math
from typing import NamedTuple

import jax
import jax.numpy as jnp
from jax import lax
from jax.experimental import pallas as pl
from jax.experimental.pallas import tpu as pltpu

F32 = jnp.float32
BF16 = jnp.bfloat16

GRID_W = 64
LN_EPS = 1e-5
SSM_CHUNK = 16
ROPE_THETA = 10000.0
CM_CHUNK = 128
WIN_H = 8
WIN_W = 16
NEG_BIAS = -1e30
LOG2E = 1.4426950408889634

LANES = 128
ROW_TILE = 256
EW_ROW_TILE = 256
VMEM_LIMIT_MB = 56

_NT = (((1,), (1,)), ((), ()))


class _Rows(NamedTuple):
    batch: int
    ctx: int
    lat: int
    with_ctx: bool

    @property
    def seg(self):
        return (self.ctx if self.with_ctx else 0) + self.lat

    @property
    def total(self):
        return self.batch * self.seg

    def mod_row(self, i, tm):
        per = self.seg // tm
        b = i // per
        if not self.with_ctx:
            return b
        return jnp.where(i % per < self.ctx // tm, self.batch, b)

    def tile_in(self, other, i, tm):
        if other.with_ctx == self.with_ctx:
            return i
        assert other.with_ctx and not self.with_ctx
        return i + (i // (self.lat // tm) + 1) * (self.ctx // tm)


def _call(kernel, *, grid, in_specs, out_specs, out_shape, scratch_shapes=(), aliases=None, name=None):
    return pl.pallas_call(
        kernel, grid=grid, in_specs=in_specs, out_specs=out_specs, out_shape=out_shape,
        scratch_shapes=list(scratch_shapes), input_output_aliases=aliases or {}, name=name,
        compiler_params=pltpu.CompilerParams(
            dimension_semantics=("arbitrary",) * len(grid),
            vmem_limit_bytes=VMEM_LIMIT_MB << 20))


def _gelu(x):
    return 0.5 * x * (1.0 + lax.erf(x * (2.0 ** -0.5)))


def _adaln_kernel(c_ref, wd_ref, wu_ref, b_ref, o_ref, t_scr):
    @pl.when(pl.program_id(1) == 0)
    def _():
        c = c_ref[...]
        t_scr[...] = jnp.dot(c * jax.nn.sigmoid(c), wd_ref[...], precision=lax.Precision.HIGHEST,
                             preferred_element_type=F32)

    o_ref[...] = jnp.dot(t_scr[...], wu_ref[...], precision=lax.Precision.HIGHEST,
                         preferred_element_type=F32) + b_ref[...]


def _adaln(cond, w_down, w_up, b):
    depth, d, r = w_down.shape
    n = w_up.shape[2]
    tn = d
    return _call(
        _adaln_kernel, grid=(depth, n // tn),
        in_specs=[pl.BlockSpec((8, d), lambda l, j: (0, 0)),
                  pl.BlockSpec((None, d, r), lambda l, j: (l, 0, 0)),
                  pl.BlockSpec((None, r, tn), lambda l, j: (l, 0, j)),
                  pl.BlockSpec((None, 1, tn), lambda l, j: (l, 0, j))],
        out_specs=pl.BlockSpec((None, 8, tn), lambda l, j: (l, 0, j)),
        out_shape=jax.ShapeDtypeStruct((depth, 8, n), F32),
        scratch_shapes=[pltpu.VMEM((8, r), F32)], name="adaln",
    )(cond, w_down, w_up, b.reshape(depth, 1, n))


def _mod_spec(rows, tm, d, chunk):
    return pl.BlockSpec((None, 1, d), lambda i: (rows.mod_row(i, tm), 0, chunk))


def _ln_res_kernel(*refs, alpha, eps, has_next):
    if has_next:
        x_ref, y_ref, gate_ref, g_ref, b_ref, sh_ref, sc_ref, xo_ref, ho_ref = refs
    else:
        x_ref, y_ref, gate_ref, g_ref, b_ref, xo_ref = refs
    z = alpha * x_ref[...] + gate_ref[...] * y_ref[...].astype(F32)
    mu = jnp.mean(z, axis=-1, keepdims=True)
    dz = z - mu
    var = jnp.mean(dz * dz, axis=-1, keepdims=True)
    xn = dz * lax.rsqrt(var + eps) * g_ref[...] + b_ref[...]
    xo_ref[...] = xn
    if has_next:
        ho_ref[...] = (xn * (1.0 + sc_ref[...]) + sh_ref[...]).astype(ho_ref.dtype)


def _ln_res(x, x_rows, y, rows, gate_mods, gate_chunk, ln_g, ln_b, alpha, next_mods=None, next_chunks=None):
    t, d = y.shape
    tm = EW_ROW_TILE
    has_next = next_mods is not None
    row_spec = pl.BlockSpec((tm, d), lambda i: (i, 0))
    vec_spec = pl.BlockSpec((1, d), lambda i: (0, 0))
    in_specs = [pl.BlockSpec((tm, d), lambda i: (rows.tile_in(x_rows, i, tm), 0)), row_spec,
                _mod_spec(rows, tm, d, gate_chunk), vec_spec, vec_spec]
    args = [x, y, gate_mods, ln_g.reshape(1, d), ln_b.reshape(1, d)]
    out_specs = [row_spec]
    out_shape = [jax.ShapeDtypeStruct((t, d), F32)]
    if has_next:
        in_specs += [_mod_spec(rows, tm, d, next_chunks[0]), _mod_spec(rows, tm, d, next_chunks[1])]
        args += [next_mods, next_mods]
        out_specs.append(row_spec)
        out_shape.append(jax.ShapeDtypeStruct((t, d), BF16))
    out = _call(
        functools.partial(_ln_res_kernel, alpha=alpha, eps=LN_EPS, has_next=has_next),
        grid=(t // tm,), in_specs=in_specs, out_specs=out_specs, out_shape=out_shape, name="ln_res",
    )(*args)
    return (out[0], out[1]) if has_next else (out[0], None)


def _swap32(a):
    lane = lax.broadcasted_iota(jnp.int32, a.shape, 1)
    return jnp.where(lane % 64 < 32, pltpu.roll(a, LANES - 32, axis=1), pltpu.roll(a, 32, axis=1))


def _mm_kernel(*refs, mode, nw, cast, q_tiles, q_scale):
    w_refs = refs[1:1 + nw]
    o_ref = refs[-1 - (nw if cast else 0)]
    if cast:
        scr = refs[len(refs) - nw:]

        @pl.when(pl.program_id(1) == 0)
        def _():
            for w_ref, s_ref in zip(w_refs, scr):
                s_ref[...] = w_ref[...].astype(s_ref.dtype)

        w_refs = scr
    x = refs[0][...]
    a = jnp.dot(x, w_refs[0][...], preferred_element_type=F32)
    if mode == "plain":
        out = a
    elif mode == "gelu":
        out = _gelu(a)
    elif mode == "swiglu":
        b = jnp.dot(x, w_refs[1][...], preferred_element_type=F32)
        out = a * jax.nn.sigmoid(a) * b
    elif mode == "glu":
        b = jnp.dot(x, w_refs[1][...], preferred_element_type=F32)
        out = a * jax.nn.sigmoid(b)
    elif mode == "rope":
        cos = refs[1 + nw][...]
        sin = refs[2 + nw][...]
        pieces = []
        for g in range(a.shape[1] // LANES):
            ag = a[:, g * LANES:(g + 1) * LANES]
            pieces.append(ag * cos + _swap32(ag) * sin)
        out = jnp.concatenate(pieces, axis=1)
    else:
        raise ValueError(mode)
    if q_tiles:
        out = out * jnp.where(pl.program_id(0) < q_tiles, q_scale, 1.0)
    o_ref[...] = out.astype(o_ref.dtype)


def _row_tile(t, k):
    budget = 24 << 20
    for nt in range(1, t // 16 + 1):
        tm = t // nt
        if t % nt == 0 and tm % 16 == 0 and 4 * tm * k <= budget:
            return tm
    raise ValueError((t, k))


def _mm(x, w, n_out, tn, *, mode="plain", layer=None, col0=0, second_col=None, rope=None, q_cols=0,
        q_scale=1.0, out_dtype=BF16):
    t, k = x.shape
    tm = _row_tile(t, k)
    assert n_out % tn == 0 and col0 % tn == 0 and q_cols % tn == 0
    cast = w.dtype != BF16
    offs = [col0 // tn] + ([second_col // tn] if mode in ("swiglu", "glu") else [])
    in_specs = [pl.BlockSpec((tm, k), lambda j, i: (i, 0))]
    if layer is None:
        in_specs += [pl.BlockSpec((k, tn), lambda j, i, off=off: (0, j + off)) for off in offs]
    else:
        in_specs += [pl.BlockSpec((None, k, tn), lambda j, i, off=off: (layer, 0, j + off)) for off in offs]
    args = [x] + [w] * len(offs)
    if mode == "rope":
        in_specs += [pl.BlockSpec((tm, LANES), lambda j, i: (i, 0))] * 2
        args += list(rope)
    return _call(
        functools.partial(_mm_kernel, mode=mode, nw=len(offs), cast=cast, q_tiles=q_cols // tn, q_scale=q_scale),
        grid=(n_out // tn, t // tm),
        in_specs=in_specs, out_specs=pl.BlockSpec((tm, tn), lambda j, i: (i, j)),
        out_shape=jax.ShapeDtypeStruct((t, n_out), out_dtype),
        scratch_shapes=[pltpu.VMEM((k, tn), BF16)] * len(offs) if cast else (), name="mm_" + mode,
    )(*args)


def _ffn(h, w_in, w_out, layer):
    f = w_out.shape[1]
    a = _mm(h, w_in, f, 256, mode="swiglu", layer=layer, second_col=f)
    return _mm(a, w_out, w_out.shape[2], 512, layer=layer)


def _cmul(ar, ai, br, bi):
    return ar * br - ai * bi, ar * bi + ai * br


def _s5_param_kernel(are_ref, aim_ref, ldt_ref, btr_ref, bti_ref, cre_ref, cim_ref,
                     kc_ref, bz_ref, coef_ref, *, q):
    for g in range(cre_ref.shape[0]):
        _s5_param_group(are_ref.at[:, g], aim_ref.at[:, g], ldt_ref.at[:, g], btr_ref.at[g], bti_ref.at[g],
                        cre_ref.at[g], cim_ref.at[g], kc_ref.at[g], bz_ref.at[g], coef_ref.at[g], q=q)


def _s5_param_group(are_ref, aim_ref, ldt_ref, btr_ref, bti_ref, cre_ref, cim_ref,
                    kc_ref, bz_ref, coef_ref, *, q):
    cre = cre_ref[...]
    cim = cim_ref[...]
    gc, p = cre.shape
    w = q * gc
    lane = lax.broadcasted_iota(jnp.int32, (gc, w), 1)
    kmat = None
    cmt = []
    bz = []
    coef_rows = []
    for d in range(2):
        lr = are_ref[d]
        li = aim_ref[d]
        dt = jnp.exp(ldt_ref[d])
        ea = jnp.exp(lr * dt)
        ar = ea * jnp.cos(li * dt)
        ai = ea * jnp.sin(li * dt)
        den = lr * lr + li * li
        nr = ar - 1.0
        kr = (nr * lr + ai * li) / den
        ki = (ai * lr - nr * li) / den
        bbr, bbi = _cmul(kr, ki, btr_ref[...], bti_ref[...])
        pr = [jnp.ones_like(ar)]
        pi = [jnp.zeros_like(ar)]
        for _ in range(q):
            nr_, ni_ = _cmul(pr[-1], pi[-1], ar, ai)
            pr.append(nr_)
            pi.append(ni_)
        cp = [_cmul(cre, cim, pr[k], pi[k]) for k in range(q + 1)]
        bp = [_cmul(bbr, bbi, pr[k], pi[k]) for k in range(q)]
        bb = jnp.concatenate([bbr, -bbi], axis=1)
        if d == 0:
            vall = jnp.concatenate([jnp.concatenate(cp[l], axis=1) for l in range(q)], axis=0)
        else:
            vall = jnp.concatenate([jnp.concatenate(cp[q - 1 - l], axis=1) for l in range(q)], axis=0)
        mt = lax.dot_general(bb, vall, _NT, precision=lax.Precision.HIGHEST,
                             preferred_element_type=F32)
        blocks = []
        for s in range(q):
            if d == 0:
                n = gc * s
                blk = mt if n == 0 else jnp.where(lane >= n, pltpu.roll(mt, n, axis=1), 0.0)
            else:
                n = gc * (q - 1 - s)
                blk = mt if n == 0 else jnp.where(lane < w - n, pltpu.roll(mt, w - n, axis=1), 0.0)
            blocks.append(blk)
        kd = jnp.concatenate(blocks, axis=0)
        kmat = kd if kmat is None else kmat + kd
        zs = [bp[q - 1 - s] if d == 0 else bp[s] for s in range(q)]
        bz.append(jnp.concatenate([jnp.concatenate([zr, zi], axis=1) for zr, zi in zs], axis=0))
        bz.append(jnp.concatenate([jnp.concatenate([zi, zr], axis=1) for zr, zi in zs], axis=0))
        cs = [cp[t + 1] if d == 0 else cp[q - t] for t in range(q)]
        cmt.append(jnp.concatenate([jnp.concatenate([vr, -vi], axis=1) for vr, vi in cs], axis=0))
        coef_rows.append(jnp.concatenate([pr[q], pr[q]], axis=1))
        coef_rows.append(jnp.concatenate([-pi[q], pi[q]], axis=1))
    kc_ref[0:w, :] = kmat.astype(kc_ref.dtype)
    cm = jnp.concatenate(cmt, axis=1).T
    kc_ref[w:w + 4 * p, :] = cm.astype(kc_ref.dtype)
    bz_ref[...] = jnp.concatenate(bz, axis=1).astype(bz_ref.dtype)
    coef_rows += [jnp.zeros_like(coef_rows[0])] * 4
    coef_ref[...] = jnp.concatenate(coef_rows, axis=0)


def _s5_params(a_re, a_im, log_dt, b_re, b_im, c_re, c_im):
    _, g, p = a_re.shape
    gc = c_re.shape[1]
    q = SSM_CHUNK
    w = q * gc
    assert 2 * p == LANES
    ng = 4
    assert g % ng == 0
    dir_spec = pl.BlockSpec((2, ng, 1, p), lambda i: (0, i, 0, 0))
    mat_spec = pl.BlockSpec((ng, gc, p), lambda i: (i, 0, 0))
    ldt = jnp.broadcast_to(log_dt[:, :, None, None], (2, g, 1, p))
    return _call(
        functools.partial(_s5_param_kernel, q=q), grid=(g // ng,),
        in_specs=[dir_spec, dir_spec, dir_spec, mat_spec, mat_spec, mat_spec, mat_spec],
        out_specs=[pl.BlockSpec((ng, w + 4 * p, w), lambda i: (i, 0, 0)),
                   pl.BlockSpec((ng, w, 8 * p), lambda i: (i, 0, 0)),
                   pl.BlockSpec((ng, 8, 2 * p), lambda i: (i, 0, 0))],
        out_shape=[jax.ShapeDtypeStruct((g, w + 4 * p, w), BF16),
                   jax.ShapeDtypeStruct((g, w, 8 * p), BF16),
                   jax.ShapeDtypeStruct((g, 8, 2 * p), F32)], name="s5_params",
    )(a_re.reshape(2, g, 1, p), a_im.reshape(2, g, 1, p), ldt,
      jnp.swapaxes(b_re, 1, 2), jnp.swapaxes(b_im, 1, 2), c_re, c_im)


def _s5_scan_kernel(u_ref, kc_ref, bz_ref, coef_ref, y_ref, *scr, nb, nsub, nsub_ctx):
    ng, nrow, w = u_ref.shape
    z_scr, s_scr = scr[:4], scr[4:]
    for g in range(ng):
        z = jnp.dot(u_ref[g], bz_ref[g], preferred_element_type=F32)
        for k in range(4):
            z_scr[k][pl.ds(g, nrow, stride=ng), :] = z[:, k * LANES:(k + 1) * LANES]
    coef = [jnp.concatenate([coef_ref[g, k:k + 1, :] for g in range(ng)], axis=0) for k in range(4)]

    def body(n, carry):
        jb = jnp.where(n < nsub_ctx, nsub_ctx - 1 - n, nsub + nsub_ctx - 1 - n)
        new = []
        k = 0
        for b in range(nb):
            for d, j in ((0, n), (1, jb)):
                h, hs = carry[k], carry[k + 1]
                k += 2
                base = pl.multiple_of((b * nsub + j) * ng, ng)
                s_scr[d][pl.ds(base, ng), :] = h
                p1, p2 = coef[2 * d], coef[2 * d + 1]
                new.append(p1 * h + p2 * hs + z_scr[2 * d][pl.ds(base, ng), :])
                new.append(p1 * hs - p2 * h + z_scr[2 * d + 1][pl.ds(base, ng), :])
        return tuple(new)

    zero = jnp.zeros((ng, LANES), F32)
    lax.fori_loop(0, nsub, body, (zero,) * (4 * nb), unroll=4)
    for g in range(ng):
        u = u_ref[g]
        st = jnp.concatenate([s_scr[d][pl.ds(g, nrow, stride=ng), :] for d in range(2)], axis=1)
        y = jnp.dot(u, kc_ref[g, 0:w, :], preferred_element_type=F32)
        y += jnp.dot(st.astype(u.dtype), kc_ref[g, w:w + 2 * LANES, :], preferred_element_type=F32)
        y_ref[g] = y.astype(y_ref.dtype)


def _group_lane_masks(shape, gc):
    lane_group = lax.broadcasted_iota(jnp.int32, shape, 1) // gc
    return [lane_group == g for g in range(LANES // gc)]


def _s5_pack_kernel(x_ref, sh_ref, sc_ref, h_ref, u_ref, h_scr, *, q, gc):
    h = x_ref[...] * (1.0 + sc_ref[...]) + sh_ref[...]
    h_ref[...] = h.astype(h_ref.dtype)
    tm, d = h.shape
    nr = tm // q
    gpb = LANES // gc
    masks = _group_lane_masks((nr, LANES), gc)
    for blk in range(d // LANES):
        h_scr[blk] = h[:, blk * LANES:(blk + 1) * LANES]
        acc = [[None] * (q // gpb) for _ in range(gpb)]
        for s in range(q):
            xs = h_scr[blk, pl.ds(s, nr, stride=q), :]
            sb, col = s % gpb, s // gpb
            for dl in range(gpb):
                g = (sb - dl) % gpb
                moved = xs if dl == 0 else pltpu.roll(xs, dl * gc, axis=1)
                prev = acc[g][col]
                acc[g][col] = jnp.where(masks[sb], moved, 0.0 if prev is None else prev)
        for g in range(gpb):
            u_ref[blk * gpb + g] = jnp.concatenate(acc[g], axis=1).astype(u_ref.dtype)


def _s5_finish_kernel(y_ref, h_ref, d_ref, o_ref, y_scr, *, q, gc):
    nr = y_ref.shape[1]
    d = h_ref.shape[1]
    gpb = LANES // gc
    masks = _group_lane_masks((nr, LANES), gc)
    for blk in range(d // LANES):
        lanes = slice(blk * LANES, (blk + 1) * LANES)
        ys = [y_ref[blk * gpb + g].astype(F32) for g in range(gpb)]
        for s in range(q):
            sb, col = s % gpb, s // gpb
            xs = None
            for dl in range(gpb):
                g = (sb + dl) % gpb
                piece = ys[g][:, col * LANES:(col + 1) * LANES]
                moved = piece if dl == 0 else pltpu.roll(piece, dl * gc, axis=1)
                xs = jnp.where(masks[g], moved, 0.0 if xs is None else xs)
            y_scr[blk, pl.ds(s, nr, stride=q), :] = xs
        v = (y_scr[blk] + d_ref[:, lanes] * h_ref[:, lanes].astype(F32)).astype(h_ref.dtype)
        o_ref[:, lanes] = _gelu(v.astype(F32)).astype(o_ref.dtype)


def _s5_mixer(x, mods, rows, a_re, a_im, log_dt, b_re, b_im, c_re, c_im, d_skip, w_glu):
    t, d = x.shape
    g = a_re.shape[1]
    gc = d // g
    q = SSM_CHUNK
    w = q * gc
    r = t // q
    assert LANES % gc == 0 and w % LANES == 0 and q % (LANES // gc) == 0
    kc, bz, coef = _s5_params(a_re, a_im, log_dt, b_re, b_im, c_re, c_im)
    tm = ROW_TILE
    nr = tm // q
    gps = 8
    assert g % gps == 0
    row_spec = pl.BlockSpec((tm, d), lambda i: (i, 0))
    grp_spec = pl.BlockSpec((g, nr, w), lambda i: (0, i, 0))
    h, u = _call(
        functools.partial(_s5_pack_kernel, q=q, gc=gc), grid=(t // tm,),
        in_specs=[row_spec, _mod_spec(rows, tm, d, 0), _mod_spec(rows, tm, d, 1)],
        out_specs=[row_spec, grp_spec],
        out_shape=[jax.ShapeDtypeStruct((t, d), BF16), jax.ShapeDtypeStruct((g, r, w), BF16)],
        scratch_shapes=[pltpu.VMEM((d // LANES, tm, LANES), F32)], name="s5_pack",
    )(x, mods, mods)
    y = _call(
        functools.partial(_s5_scan_kernel, nb=rows.batch, nsub=rows.seg // q, nsub_ctx=rows.ctx // q),
        grid=(g // gps,),
        in_specs=[pl.BlockSpec((gps, r, w), lambda i: (i, 0, 0)),
                  pl.BlockSpec((gps,) + kc.shape[1:], lambda i: (i, 0, 0)),
                  pl.BlockSpec((gps,) + bz.shape[1:], lambda i: (i, 0, 0)),
                  pl.BlockSpec((gps,) + coef.shape[1:], lambda i: (i, 0, 0))],
        out_specs=pl.BlockSpec((gps, r, w), lambda i: (i, 0, 0)),
        out_shape=jax.ShapeDtypeStruct((g, r, w), BF16),
        scratch_shapes=[pltpu.VMEM((r * gps, LANES), F32)] * 6,
        name="s5_scan",
    )(u, kc, bz, coef)
    v = _call(
        functools.partial(_s5_finish_kernel, q=q, gc=gc), grid=(t // tm,),
        in_specs=[grp_spec, row_spec, pl.BlockSpec((1, d), lambda i: (0, 0))],
        out_specs=row_spec, out_shape=jax.ShapeDtypeStruct((t, d), BF16),
        scratch_shapes=[pltpu.VMEM((d // LANES, tm, LANES), F32)], name="s5_finish",
    )(y, h, d_skip.reshape(1, d))
    return _mm(v, w_glu, d, 256, mode="glu", second_col=d)


def _rope_tables(rows):
    half = LANES // 2
    quarter = half // 2
    tpos = jnp.arange(rows.lat)
    pos = jnp.stack([tpos // GRID_W, tpos % GRID_W], axis=1).astype(F32)
    inv = ROPE_THETA ** (-jnp.arange(quarter, dtype=F32) * 2.0 / half)
    ang = pos[:, :, None] * inv[None, None, :]
    ang = jnp.repeat(ang[:, :, None, :], 2, axis=2).reshape(rows.lat, LANES)
    sign = jnp.tile(jnp.repeat(jnp.array([-1.0, 1.0], F32), quarter), 2)
    cos = jnp.cos(ang)
    sin = jnp.sin(ang) * sign
    cos = jnp.concatenate([jnp.ones((rows.ctx, LANES), F32), cos], axis=0)
    sin = jnp.concatenate([jnp.zeros((rows.ctx, LANES), F32), sin], axis=0)
    return jnp.tile(cos, (rows.batch, 1)), jnp.tile(sin, (rows.batch, 1))


def _da_kernel(lam_ref, g_ref, q_ref, k_ref, v_ref, o_ref, s_scr, e_scr, *, lam_init, eps, tk, nctx):
    @pl.when(pl.program_id(2) == 0)
    def _():
        _da_attend(lam_ref, g_ref, q_ref, k_ref, v_ref, o_ref, s_scr, e_scr, nctx,
                   lam_init=lam_init, eps=eps, tk=tk)

    @pl.when(pl.program_id(2) != 0)
    def _():
        _da_attend(lam_ref, g_ref, q_ref, k_ref, v_ref, o_ref, s_scr, e_scr, k_ref.shape[0],
                   lam_init=lam_init, eps=eps, tk=tk)


def _da_attend(lam_ref, g_ref, q_ref, k_ref, v_ref, o_ref, s_scr, e_scr, nk, *, lam_init, eps, tk):
    lp = lam_ref[...]
    dh = lp.shape[1]
    lam = (jnp.exp(jnp.sum(lp[0:1] * lp[1:2], axis=-1, keepdims=True))
           - jnp.exp(jnp.sum(lp[2:3] * lp[3:4], axis=-1, keepdims=True)) + lam_init)
    tq = q_ref.shape[0]
    nchunk = nk // tk
    nunit = q_ref.shape[1] // dh
    row_max = []
    for u in range(nunit):
        qu = q_ref[:, u * dh:(u + 1) * dh]
        m = jnp.full((tq, LANES), -jnp.inf, F32)
        for c in range(nchunk):
            s = lax.dot_general(qu, k_ref[c * tk:(c + 1) * tk, u * dh:(u + 1) * dh], _NT,
                                preferred_element_type=F32)
            s_scr[u, :, c * tk:(c + 1) * tk] = s
            for j in range(tk // LANES):
                m = jnp.maximum(m, s[:, j * LANES:(j + 1) * LANES])
        row_max.append(jnp.max(m, axis=-1, keepdims=True))
    outs = []
    for u in range(nunit):
        mrow = row_max[u]
        den = jnp.zeros((tq, LANES), F32)
        for c in range(nchunk):
            e = jnp.exp2(s_scr[u, :, c * tk:(c + 1) * tk] - mrow)
            for j in range(tk // LANES):
                den = den + e[:, j * LANES:(j + 1) * LANES]
            e_scr[u, :, c * tk:(c + 1) * tk] = e.astype(e_scr.dtype)
        den = jnp.sum(den, axis=-1, keepdims=True)
        hd = u // 2
        outs.append(jnp.dot(e_scr[u, :, 0:nk], v_ref[0:nk, hd * 2 * dh:(hd + 1) * 2 * dh],
                            preferred_element_type=F32) * (1.0 / den))
    for hd in range(nunit // 2):
        o = outs[2 * hd] - lam * outs[2 * hd + 1]
        on = o * lax.rsqrt(jnp.mean(o * o, axis=-1, keepdims=True) + eps)
        o_ref[:, hd * 2 * dh:(hd + 1) * 2 * dh] = (on * g_ref[...] * (1.0 - lam_init)).astype(o_ref.dtype)


def _diff_attention(h, rows, w_qkv, w_o, lam_p, subln_g, lam_init):
    t, d = h.shape
    dh = lam_p.shape[1]
    hb = 2
    hw = hb * 2 * dh
    heads = d // hw
    cos, sin = _rope_tables(rows)
    qk = _mm(h, w_qkv, 2 * d, 512, mode="rope", rope=(cos, sin), q_cols=d, q_scale=dh ** -0.5 * LOG2E)
    v = _mm(h, w_qkv, d, 512, col0=2 * d)
    tk = ROW_TILE
    tq = ROW_TILE
    s, c = rows.seg, rows.ctx
    assert c == tq and s % tq == 0, "the context prefix is exactly the first query tile of a batch element"
    o = _call(
        functools.partial(_da_kernel, lam_init=lam_init, eps=LN_EPS, tk=tk, nctx=c),
        grid=(rows.batch, heads, s // tq),
        in_specs=[pl.BlockSpec(lam_p.shape, lambda b, hd, i: (0, 0)),
                  pl.BlockSpec((1, 2 * dh), lambda b, hd, i: (0, 0)),
                  pl.BlockSpec((tq, hw), lambda b, hd, i: (b * (s // tq) + i, hd)),
                  pl.BlockSpec((s, hw), lambda b, hd, i: (b, heads + hd)),
                  pl.BlockSpec((s, hw), lambda b, hd, i: (b, hd))],
        out_specs=pl.BlockSpec((tq, hw), lambda b, hd, i: (b * (s // tq) + i, hd)),
        out_shape=jax.ShapeDtypeStruct((t, d), BF16),
        scratch_shapes=[pltpu.VMEM((2 * hb, tq, s), F32), pltpu.VMEM((2 * hb, tq, s), BF16)],
        name="diff_attn",
    )(lam_p, subln_g.reshape(1, 2 * dh), qk, qk, v)
    return _mm(o, w_o, d, 512)


def _cm_kernel(u_ref, v_ref, g_ref, b_ref, ws_ref, bs_ref, o_ref, *, eps):
    v = v_ref[...].astype(F32)
    mu = jnp.mean(v, axis=-1, keepdims=True)
    dv = v - mu
    var = jnp.mean(dv * dv, axis=-1, keepdims=True)
    vn = (dv * lax.rsqrt(var + eps) * g_ref[...] + b_ref[...]).astype(ws_ref.dtype)
    groups = ws_ref.shape[0]
    gw = vn.shape[1] // groups
    for g in range(groups):
        cols = slice(g * gw, (g + 1) * gw)
        vm = jnp.dot(ws_ref[g], vn[:, cols], preferred_element_type=F32) + bs_ref[:, g:g + 1]
        o_ref[:, cols] = (u_ref[:, cols].astype(F32) * vm).astype(o_ref.dtype)


def _chunk_mlp(h, w_in, ln_g, ln_b, w_s, b_s, w_out):
    t, d = h.shape
    uv = _mm(h, w_in, 2 * d, 512, mode="gelu")
    tc = CM_CHUNK
    groups = w_s.shape[0]
    vec_spec = pl.BlockSpec((1, d), lambda i: (0, 0))
    gated = _call(
        functools.partial(_cm_kernel, eps=LN_EPS), grid=(t // tc,),
        in_specs=[pl.BlockSpec((tc, d), lambda i: (i, 0)), pl.BlockSpec((tc, d), lambda i: (i, 1)),
                  vec_spec, vec_spec,
                  pl.BlockSpec((groups, tc, tc), lambda i: (0, 0, 0)),
                  pl.BlockSpec((tc, groups), lambda i: (0, 0))],
        out_specs=pl.BlockSpec((tc, d), lambda i: (i, 0)),
        out_shape=jax.ShapeDtypeStruct((t, d), BF16), name="chunk_gate",
    )(uv, uv, ln_g.reshape(1, d), ln_b.reshape(1, d), w_s.astype(BF16), b_s.T)
    return _mm(gated, w_out, d, 512)


def _na_bias_kernel(rpb_ref, o_ref, *, kh):
    cq = lax.broadcasted_iota(jnp.int32, (GRID_W, GRID_W), 0)
    ck = lax.broadcasted_iota(jnp.int32, (GRID_W, GRID_W), 1)
    cs = jnp.clip(cq - WIN_W // 2, 0, GRID_W - WIN_W)
    valid = jnp.logical_and(ck >= cs, ck < cs + WIN_W)
    tiles = []
    for dr in range(2 * WIN_H - 1):
        v = jnp.broadcast_to(rpb_ref[dr:dr + 1, :], (GRID_W, LANES)) * LOG2E
        t = pltpu.roll(v, LANES - (WIN_W - 1), axis=1, stride=1, stride_axis=0)[:, :GRID_W]
        tiles.append(jnp.where(valid, t, NEG_BIAS))
    for off in range(kh):
        o_ref[off] = jnp.concatenate([tiles[i - off + WIN_H - 1] for i in range(kh)], axis=1)


def _na_bias_table(rpb, kh):
    heads, nr, nc = rpb.shape
    assert nr <= 16 and nc <= LANES and GRID_W + WIN_W <= LANES
    padded = jnp.zeros((heads, 16, LANES), F32).at[:, :nr, :nc].set(rpb.astype(F32))
    return _call(
        functools.partial(_na_bias_kernel, kh=kh), grid=(heads,),
        in_specs=[pl.BlockSpec((None, 16, LANES), lambda hd: (hd, 0, 0))],
        out_specs=pl.BlockSpec((kh, None, GRID_W, kh * GRID_W), lambda hd: (0, hd, 0, 0)),
        out_shape=jax.ShapeDtypeStruct((kh, heads, GRID_W, kh * GRID_W), F32), name="nbr_bias",
    )(padded)


def _na_kernel(*refs, dh, kh, nrows, ctx, rq):
    bm_refs = refs[:rq]
    q_ref, k_ref, v_ref, o_ref = refs[rq:]
    nloc = kh * GRID_W
    chains = []
    for qi in range(rq):
        r = pl.program_id(2) * rq + qi
        rs = jnp.clip(r - kh // 2, 0, nrows - kh)
        start = pl.multiple_of(ctx + rs * GRID_W, GRID_W)
        qrows = slice(qi * GRID_W, (qi + 1) * GRID_W)
        for hd in range(q_ref.shape[1] // dh):
            cols = slice(hd * dh, (hd + 1) * dh)
            qh = q_ref[qrows, cols]
            sl = lax.dot_general(qh, k_ref[pl.ds(start, nloc), cols], _NT,
                                 preferred_element_type=F32) + bm_refs[qi][hd]
            sc = lax.dot_general(qh, k_ref[0:ctx, cols], _NT, preferred_element_type=F32)
            chains.append((qrows, cols, start, sl, sc))
    probs = []
    for _, _, _, sl, sc in chains:
        m = jnp.maximum(jnp.max(sl, axis=-1, keepdims=True), jnp.max(sc, axis=-1, keepdims=True))
        el = jnp.exp2(sl - m)
        ec = jnp.exp2(sc - m)
        den = jnp.sum(el, axis=-1, keepdims=True) + jnp.sum(ec, axis=-1, keepdims=True)
        probs.append((el.astype(v_ref.dtype), ec.astype(v_ref.dtype), 1.0 / den))
    for (qrows, cols, start, _, _), (el, ec, inv) in zip(chains, probs):
        o = (jnp.dot(el, v_ref[pl.ds(start, nloc), cols], preferred_element_type=F32)
             + jnp.dot(ec, v_ref[0:ctx, cols], preferred_element_type=F32))
        o_ref[qrows, cols] = (o * inv).astype(o_ref.dtype)


def _neighbourhood_attention(h, rows, w_qkv, w_o, rpb):
    t, d = h.shape
    heads = rpb.shape[0]
    dh = d // heads
    nrows = rows.lat // GRID_W
    kh = min(WIN_H, nrows)
    qkv = _mm(h, w_qkv, 3 * d, 512, q_cols=d, q_scale=dh ** -0.5 * LOG2E)
    bm = _na_bias_table(rpb, kh)
    hg = 4
    rq = 4
    hw = hg * dh
    s, c = rows.seg, rows.ctx
    assert nrows % rq == 0 and s % (rq * GRID_W) == 0 and c % (rq * GRID_W) == 0

    def bm_spec(qi):
        def index(b, g, r):
            row = r * rq + qi
            return (row - jnp.clip(row - kh // 2, 0, nrows - kh), g, 0, 0)
        return pl.BlockSpec((None, hg, GRID_W, kh * GRID_W), index)

    o = _call(
        functools.partial(_na_kernel, dh=dh, kh=kh, nrows=nrows, ctx=c, rq=rq),
        grid=(rows.batch, heads // hg, nrows // rq),
        in_specs=[bm_spec(qi) for qi in range(rq)] + [
            pl.BlockSpec((rq * GRID_W, hw), lambda b, g, r: ((b * s + c) // (rq * GRID_W) + r, g)),
            pl.BlockSpec((s, hw), lambda b, g, r: (b, d // hw + g)),
            pl.BlockSpec((s, hw), lambda b, g, r: (b, 2 * d // hw + g))],
        out_specs=pl.BlockSpec((rq * GRID_W, hw), lambda b, g, r: (b * (nrows // rq) + r, g)),
        out_shape=jax.ShapeDtypeStruct((rows.batch * rows.lat, d), BF16), name="nbr_attn",
    )(*([bm] * rq), qkv, qkv, qkv)
    return _mm(o, w_o, d, 512)


def kernel(x, c, ctx, c_ctx, ada_down, ada_up, ada_b, ln_g, ln_b, ffn_w_in, ffn_w_out,
           ssm_a_re, ssm_a_im, ssm_log_dt, ssm_b_re, ssm_b_im, ssm_c_re, ssm_c_im, ssm_d,
           ssm_w_glu, da_w_qkv, da_w_o, da_lambda, da_subln_g,
           cm_w_in, cm_ln_g, cm_ln_b, cm_w_s, cm_b_s, cm_w_out,
           na_w_qkv, na_w_o, na_rpb):
    bsz, seq, d = x.shape
    nctx = ctx.shape[1]
    depth = ada_down.shape[0]
    assert depth == 4 and bsz + 1 <= 8, "one layer per mixer kind; the last layer drops the context stream"
    assert nctx % ROW_TILE == 0 and seq % ROW_TILE == 0 and seq % GRID_W == 0
    assert nctx % EW_ROW_TILE == 0 and seq % EW_ROW_TILE == 0
    alpha = (2 * depth) ** 0.25

    cond = jnp.zeros((8, d), F32).at[:bsz].set(c).at[bsz].set(c_ctx)
    mods = _adaln(cond, ada_down, ada_up, ada_b).reshape(depth, 8, 1, 6 * d)
    uni = _Rows(bsz, nctx, seq, True)
    lat = _Rows(bsz, nctx, seq, False)

    xs = jnp.concatenate([ctx, x], axis=1).reshape(uni.total, d)
    w_out_bf16 = ffn_w_out.astype(BF16)
    h = None
    for i in range(depth):
        rows = uni
        if i == 0:
            y = _s5_mixer(xs, mods[0], uni, ssm_a_re[0], ssm_a_im[0], ssm_log_dt[0], ssm_b_re[0], ssm_b_im[0],
                          ssm_c_re[0], ssm_c_im[0], ssm_d[0], ssm_w_glu[0])
        elif i == 1:
            y = _diff_attention(h, uni, da_w_qkv[0], da_w_o[0], da_lambda[0], da_subln_g[0],
                                0.8 - 0.6 * math.exp(-0.3 * i))
        elif i == 2:
            y = _chunk_mlp(h, cm_w_in[0], cm_ln_g[0], cm_ln_b[0], cm_w_s[0], cm_b_s[0], cm_w_out[0])
        else:
            y = _neighbourhood_attention(h, uni, na_w_qkv[0], na_w_o[0], na_rpb[0])
            rows = lat
        xs, h = _ln_res(xs, uni, y, rows, mods[i], 2, ln_g[i, 0], ln_b[i, 0], alpha,
                        next_mods=mods[i], next_chunks=(3, 4))
        y = _ffn(h, ffn_w_in, w_out_bf16, i)
        if i + 1 < depth:
            xs, h = _ln_res(xs, rows, y, rows, mods[i], 5, ln_g[i, 1], ln_b[i, 1], alpha,
                            next_mods=mods[i + 1], next_chunks=(0, 1))
        else:
            xs, _ = _ln_res(xs, rows, y, rows, mods[i], 5, ln_g[i, 1], ln_b[i, 1], alpha)
    return xs.reshape(bsz, seq, d)
```
